```python
import math
import jax, jax.numpy as jnp
from jax import lax
import numpy as np

D_MODEL = 2048
BATCH = 2
SEQ = 16384
DEPTH = 1

CHUNK = 64
QBLOCK = 128
ATT_HEADS = 8
ATT_HEAD_DIM = 128
ATT_WIDTH = ATT_HEADS * ATT_HEAD_DIM
IDX_HEADS = 16
IDX_DIM = 64
MAX_TOPK = 256
CONV_WIDTH = 1024
CONV_K = 3
REL_BUCKETS = 32
REL_MAX_DIST = 128
N_EXPERTS = 32
TOP_K_EXPERTS = 4
D_FF = D_MODEL
SWIGLU_LIMIT = 7.0
SWIGLU_ALPHA = 1.702
EXPERT_BLOCK = 256
PLE_DIM = 256
LN_EPS = 1e-5
DEEPNORM_ALPHA = (2 * DEPTH) ** 0.25
DEEPNORM_BETA = (8 * DEPTH) ** -0.25

IN_SPLITS = [ATT_WIDTH, ATT_WIDTH, ATT_WIDTH, IDX_HEADS * IDX_DIM, IDX_DIM, IDX_HEADS,
             CONV_WIDTH, CONV_WIDTH, CONV_WIDTH, D_MODEL, D_MODEL]
IN_COLS = sum(IN_SPLITS)
SPLIT_POINTS = [int(v) for v in np.cumsum(IN_SPLITS)[:-1]]

kernel_name = 'hybrid_dsa_shortconv_moe_block'


def layer_norm(x, g, b):
    xf = x.astype(jnp.float32)
    mu = jnp.mean(xf, axis=-1, keepdims=True)
    xc = xf - mu
    var = jnp.mean(xc * xc, axis=-1, keepdims=True)
    y = xc * lax.rsqrt(var + LN_EPS) * g.astype(jnp.float32) + b.astype(jnp.float32)
    return y.astype(x.dtype)


def t5_bucket(rel):
    half = REL_BUCKETS // 2
    max_exact = half // 2
    ret = jnp.where(rel > 0, half, 0)
    n = jnp.abs(rel)
    nf = jnp.maximum(n, 1).astype(jnp.float32)
    large = max_exact + (jnp.log(nf / max_exact) / math.log(REL_MAX_DIST / max_exact)
                         * (half - max_exact)).astype(jnp.int32)
    large = jnp.minimum(large, half - 1)
    return ret + jnp.where(n < max_exact, n, large)


def sparse_attention(q, k, v, qi, ki, wi, rel_bias):
    bsz, seq = q.shape[0], q.shape[1]
    topk = min(MAX_TOPK, seq // 4)
    nblk = seq // QBLOCK
    key_pos = jnp.arange(seq, dtype=jnp.int32)
    bidx = jnp.arange(bsz)[:, None, None]
    scale = ATT_HEAD_DIM ** -0.5

    def to_blocks(a):
        return jnp.moveaxis(a.reshape((bsz, nblk, QBLOCK) + a.shape[2:]), 1, 0)

    def block(args):
        blk, qb, qib, wib = args
        t = blk * QBLOCK + jnp.arange(QBLOCK, dtype=jnp.int32)
        limit = (t // CHUNK + 1) * CHUNK
        visible = key_pos[None, :] < limit[:, None]
        idx_logit = jnp.einsum('bqhd,bsd->bqhs', qib, ki)
        idx_score = jnp.einsum('bqhs,bqh->bqs', jax.nn.relu(idx_logit), wib).astype(jnp.float32)
        idx_score = jnp.where(visible[None], idx_score, -jnp.inf)
        _, sel = lax.top_k(idx_score, topk)
        sel_valid = sel < limit[None, :, None]
        kg = k[bidx, sel]
        vg = v[bidx, sel]
        bias = rel_bias[t5_bucket(sel - t[None, :, None])]
        logits = (jnp.einsum('bqhd,bqkhd->bqhk', qb, kg).astype(jnp.float32) * scale
                  + jnp.transpose(bias, (0, 1, 3, 2)).astype(jnp.float32))
        logits = jnp.where(sel_valid[:, :, None, :], logits, -1e30)
        probs = jax.nn.softmax(logits, axis=-1).astype(vg.dtype)
        o = jnp.einsum('bqhk,bqkhd->bqhd', probs, vg)
        return o.reshape(bsz, QBLOCK, ATT_WIDTH)

    out = lax.map(block, (jnp.arange(nblk, dtype=jnp.int32), to_blocks(q), to_blocks(qi), to_blocks(wi)))
    return jnp.moveaxis(out, 0, 1).reshape(bsz, seq, ATT_WIDTH)


def short_gated_conv(gate_b, gate_c, u, conv_w):
    cu = gate_c * u
    y = lax.conv_general_dilated(cu, conv_w[:, None, :], window_strides=(1,),
                                 padding=[(CONV_K - 1, 0)],
                                 dimension_numbers=('NWC', 'WIO', 'NWC'),
                                 feature_group_count=CONV_WIDTH)
    return gate_b * y


def token_mixers(h, w_in, conv_w, rel_bias, w_att_br, w_conv_br, w_o):
    bsz, seq, _ = h.shape
    z = h @ w_in
    q, k, v, qi, ki, wi, gb, gc, u, ga, gcv = jnp.split(z, SPLIT_POINTS, axis=-1)
    hd = (bsz, seq, ATT_HEADS, ATT_HEAD_DIM)
    att = sparse_attention(q.reshape(hd), k.reshape(hd), v.reshape(hd),
                           qi.reshape(bsz, seq, IDX_HEADS, IDX_DIM), ki, wi, rel_bias)
    conv = short_gated_conv(gb, gc, u, conv_w)
    merged = jax.nn.sigmoid(ga) * (att @ w_att_br) + jax.nn.sigmoid(gcv) * (conv @ w_conv_br)
    return merged @ w_o


def moe(h, w_router, b_router, w_gate, b_gate, w_up, b_up, w_down, b_down):
    bsz, seq, d = h.shape
    n_tok = bsz * seq
    n_assign = n_tok * TOP_K_EXPERTS
    xf = h.reshape(n_tok, d)
    logits = (xf @ w_router + b_router).astype(jnp.float32)
    top_v, top_e = lax.top_k(logits, TOP_K_EXPERTS)
    gates = jax.nn.softmax(top_v, axis=-1)
    e_flat = top_e.reshape(-1)
    tok_flat = jnp.repeat(jnp.arange(n_tok, dtype=jnp.int32), TOP_K_EXPERTS)
    g_flat = gates.reshape(-1)
    order = jnp.argsort(e_flat)
    e_sorted = e_flat[order]
    counts = jnp.bincount(e_flat, length=N_EXPERTS)
    starts = jnp.cumsum(counts) - counts
    padded = (counts + EXPERT_BLOCK - 1) // EXPERT_BLOCK * EXPERT_BLOCK
    pends = jnp.cumsum(padded)
    pstarts = pends - padded
    dest = pstarts[e_sorted] + (jnp.arange(n_assign) - starts[e_sorted])
    n_blocks = (n_assign + N_EXPERTS * (EXPERT_BLOCK - 1) + EXPERT_BLOCK - 1) // EXPERT_BLOCK
    n_rows = n_blocks * EXPERT_BLOCK
    row_tok = jnp.zeros((n_rows,), jnp.int32).at[dest].set(tok_flat[order])
    row_w = jnp.zeros((n_rows,), jnp.float32).at[dest].set(g_flat[order])
    block_start = jnp.arange(n_blocks) * EXPERT_BLOCK
    block_e = jnp.minimum(jnp.searchsorted(pends, block_start, side='right'), N_EXPERTS - 1)

    def step(y, blk):
        tok, wt, e = blk
        xb = xf[tok]
        g = xb @ w_gate[e] + b_gate[e]
        u = xb @ w_up[e] + b_up[e]
        g = jnp.minimum(g, SWIGLU_LIMIT)
        u = jnp.clip(u, -SWIGLU_LIMIT, SWIGLU_LIMIT)
        a = (u + 1.0) * (g * jax.nn.sigmoid(SWIGLU_ALPHA * g))
        o = a @ w_down[e] + b_down[e]
        return y.at[tok].add(o * wt.astype(o.dtype)[:, None]), None

    y, _ = lax.scan(step, jnp.zeros_like(xf),
                    (row_tok.reshape(n_blocks, EXPERT_BLOCK),
                     row_w.reshape(n_blocks, EXPERT_BLOCK), block_e))
    return y.reshape(bsz, seq, d)


def setup_inputs(seed: int = 0) -> dict:
    key = jax.random.key(seed)
    ks = jax.random.split(key, 32)
    L, D, E, F = DEPTH, D_MODEL, N_EXPERTS, D_FF

    def nrm(k, shape, scale):
        return jax.random.normal(k, shape, jnp.float32) * scale

    return {
        'x': nrm(ks[0], (BATCH, SEQ, D), 1.0),
        'p': nrm(ks[1], (DEPTH, BATCH, SEQ, PLE_DIM), 1.0),
        'ln_in_g': 1.0 + nrm(ks[2], (D,), 0.02),
        'ln_in_b': nrm(ks[3], (D,), 0.02),
        'rel_bias': nrm(ks[4], (REL_BUCKETS, ATT_HEADS), 0.5),
        'w_in': nrm(ks[5], (L, D, IN_COLS), D ** -0.5),
        'conv_w': nrm(ks[6], (L, CONV_K, CONV_WIDTH), CONV_K ** -0.5),
        'w_att_br': nrm(ks[7], (L, ATT_WIDTH, D), ATT_WIDTH ** -0.5 * DEEPNORM_BETA),
        'w_conv_br': nrm(ks[8], (L, CONV_WIDTH, D), CONV_WIDTH ** -0.5 * DEEPNORM_BETA),
        'w_o': nrm(ks[9], (L, D, D), D ** -0.5 * DEEPNORM_BETA),
        'ln1_g': 1.0 + nrm(ks[10], (L, D), 0.02),
        'ln1_b': nrm(ks[11], (L, D), 0.02),
        'w_router': nrm(ks[12], (L, D, E), D ** -0.5),
        'b_router': nrm(ks[13], (L, E), 0.01),
        'w_gate': nrm(ks[14], (L, E, D, F), D ** -0.5),
        'b_gate': nrm(ks[15], (L, E, F), 0.01),
        'w_up': nrm(ks[16], (L, E, D, F), D ** -0.5),
        'b_up': nrm(ks[17], (L, E, F), 0.01),
        'w_down': nrm(ks[18], (L, E, F, D), F ** -0.5 * DEEPNORM_BETA),
        'b_down': nrm(ks[19], (L, E, D), 0.01),
        'w_ple_gate': nrm(ks[20], (L, D, D), D ** -0.5),
        'w_ple_proj': nrm(ks[21], (L, PLE_DIM, D), PLE_DIM ** -0.5),
        'ln2_g': 1.0 + nrm(ks[22], (L, D), 0.02),
        'ln2_b': nrm(ks[23], (L, D), 0.02),
    }


def reference(x, p, ln_in_g, ln_in_b, rel_bias, w_in, conv_w, w_att_br, w_conv_br, w_o,
              ln1_g, ln1_b, w_router, b_router, w_gate, b_gate, w_up, b_up, w_down, b_down,
              w_ple_gate, w_ple_proj, ln2_g, ln2_b):
    h = layer_norm(x, ln_in_g, ln_in_b)
    for i in range(DEPTH):
        mix = token_mixers(h, w_in[i], conv_w[i], rel_bias, w_att_br[i], w_conv_br[i], w_o[i])
        h = layer_norm(DEEPNORM_ALPHA * h + mix, ln1_g[i], ln1_b[i])
        ff = moe(h, w_router[i], b_router[i], w_gate[i], b_gate[i], w_up[i], b_up[i],
                 w_down[i], b_down[i])
        ple = jax.nn.sigmoid(h @ w_ple_gate[i]) * (p[i] @ w_ple_proj[i])
        h = layer_norm(DEEPNORM_ALPHA * h + ff + ple, ln2_g[i], ln2_b[i])
    return h
```

```python
import functools
import math

import jax
import jax.numpy as jnp
from jax import lax
from jax.experimental import pallas as pl
from jax.experimental.pallas import tpu as pltpu

CHUNK = 64
CHUNK_SHIFT = 6
assert 1 << CHUNK_SHIFT == CHUNK
ATT_HEADS = 8
ATT_HEAD_DIM = 128
ATT_WIDTH = ATT_HEADS * ATT_HEAD_DIM
IDX_HEADS = 16
IDX_DIM = 64
MAX_TOPK = 256
CONV_WIDTH = 1024
CONV_K = 3
REL_BUCKETS = 32
REL_MAX_DIST = 128
TOP_K_EXPERTS = 4
SWIGLU_LIMIT = 7.0
SWIGLU_ALPHA = 1.702
LN_EPS = 1e-5
DEPTH = 1
DEEPNORM_ALPHA = (2 * DEPTH) ** 0.25

LANES = 128
VMEM_LIMIT_BYTES = 56 * 1024 * 1024

INT_MIN = -(2 ** 31)
KEY_NEG_INF = (0xFF800000 ^ 0x7FFFFFFF) - (1 << 32)

F32 = jnp.float32
BF16 = jnp.bfloat16


def _cparams(sem):
    return pltpu.CompilerParams(dimension_semantics=sem, vmem_limit_bytes=VMEM_LIMIT_BYTES)


def _ln(xf, g, b):
    mu = jnp.mean(xf, axis=-1, keepdims=True)
    xc = xf - mu
    var = jnp.mean(xc * xc, axis=-1, keepdims=True)
    return xc * lax.rsqrt(var + LN_EPS) * g + b


def _rep_lanes(v, rep):
    return v if rep == 1 else jnp.concatenate([v] * rep, axis=1)


def _resident(shape, index_map):
    return pl.BlockSpec(shape, index_map, pipeline_mode=pl.Buffered(1))


def _bias_tab_kernel(rb_ref, o_ref, *, T):
    c = pl.program_id(0)
    h = pl.program_id(1)
    half = REL_BUCKETS // 2
    max_exact = half // 2
    i = lax.broadcasted_iota(jnp.int32, (T, T), 0)
    j = lax.broadcasted_iota(jnp.int32, (T, T), 1)
    rel = j - i - c * T
    ret = jnp.where(rel > 0, half, 0)
    n = jnp.abs(rel)
    nf = jnp.maximum(n, 1).astype(F32)
    large = max_exact + (jnp.log(nf / max_exact) / math.log(REL_MAX_DIST / max_exact)
                         * (half - max_exact)).astype(jnp.int32)
    large = jnp.minimum(large, half - 1)
    bucket = ret + jnp.where(n < max_exact, n, large)
    far = rb_ref[half - 1, h]
    acc = jnp.zeros((T, T), F32)
    for b in range(REL_BUCKETS):
        acc = jnp.where(bucket == b, rb_ref[b, h] - far, acc)
    o_ref[...] = acc


def _bias_tab(rel_bias, T):
    return pl.pallas_call(
        functools.partial(_bias_tab_kernel, T=T),
        grid=(2, ATT_HEADS),
        in_specs=[pl.BlockSpec(memory_space=pltpu.SMEM)],
        out_specs=pl.BlockSpec((None, None, T, T), lambda c, h: (c, h, 0, 0)),
        out_shape=jax.ShapeDtypeStruct((2, ATT_HEADS, T, T), F32),
        compiler_params=_cparams(("arbitrary", "arbitrary")),
        name="bias_tab",
    )(rel_bias)


def _inproj_kernel(x_ref, g_ref, b_ref, w_ref, ws_ref, z_ref, zs_ref, hn_sc):
    j = pl.program_id(1)

    @pl.when(j == 0)
    def _():
        hn = _ln(x_ref[...], g_ref[...], b_ref[...]).astype(BF16)
        hn_sc[...] = hn
        zs_ref[...] = jnp.dot(hn, ws_ref[...], preferred_element_type=F32)

    z_ref[...] = jnp.dot(hn_sc[...], w_ref[...], preferred_element_type=F32).astype(BF16)


def _inproj(x2, g, b, w_main, w_small, tm, tn):
    M, D = x2.shape
    N = w_main.shape[1]
    return pl.pallas_call(
        _inproj_kernel,
        grid=(M // tm, N // tn),
        in_specs=[
            pl.BlockSpec((tm, D), lambda i, j: (i, 0)),
            pl.BlockSpec((1, D), lambda i, j: (0, 0)),
            pl.BlockSpec((1, D), lambda i, j: (0, 0)),
            pl.BlockSpec((D, tn), lambda i, j: (0, j)),
            pl.BlockSpec((D, LANES), lambda i, j: (0, 0)),
        ],
        out_specs=[
            pl.BlockSpec((tm, tn), lambda i, j: (i, j)),
            pl.BlockSpec((tm, LANES), lambda i, j: (i, 0)),
        ],
        out_shape=[
            jax.ShapeDtypeStruct((M, N), BF16),
            jax.ShapeDtypeStruct((M, LANES), F32),
        ],
        scratch_shapes=[pltpu.VMEM((tm, D), BF16)],
        compiler_params=_cparams(("parallel", "arbitrary")),
        name="ln_inproj",
    )(x2, g, b, w_main, w_small)


def _attn_kernel(q_ref, qi_ref, zs_ref, kite_ref, kito_ref, k_ref, v_ref, tab_ref, o_ref,
                 key_sc, wb_sc, thr_sc, m_sc, l_sc, acc_sc, *, TQ, TK, CK, topk):
    qb = pl.program_id(1)
    kt = pl.program_id(2)
    t0 = qb * TQ
    lim_max = t0 + TQ
    last_kt = (lim_max - 1) // TK
    n_sub = TK // TQ
    scale = ATT_HEAD_DIM ** -0.5

    @pl.when(kt == 0)
    def _phase1():
        for h in range(IDX_HEADS):
            col = zs_ref[:, IDX_DIM + h:IDX_DIM + h + 1]
            wb_sc[h] = jnp.broadcast_to(col, (TQ, LANES))
        n_ck = (lim_max + CK - 1) // CK
        rep = CK // LANES

        def score_chunk(c, carry):
            c0 = pl.multiple_of(c * CK, CK)
            acc = jnp.zeros((TQ, CK), F32)
            for jp in range(IDX_HEADS // 2):
                qpair = qi_ref[:, LANES * jp:LANES * (jp + 1)]
                le = jnp.dot(qpair, kite_ref[:, pl.ds(c0, CK)], preferred_element_type=F32)
                lo = jnp.dot(qpair, kito_ref[:, pl.ds(c0, CK)], preferred_element_type=F32)
                acc = acc + jnp.maximum(le, 0.0) * _rep_lanes(wb_sc[2 * jp], rep)
                acc = acc + jnp.maximum(lo, 0.0) * _rep_lanes(wb_sc[2 * jp + 1], rep)
            s_idx = c0 + lax.broadcasted_iota(jnp.int32, (TQ, CK), 1)
            t_idx = t0 + lax.broadcasted_iota(jnp.int32, (TQ, CK), 0)
            limit = ((t_idx >> CHUNK_SHIFT) + 1) << CHUNK_SHIFT
            sc = jnp.where(s_idx < limit, acc, -jnp.inf)
            bits = pltpu.bitcast(sc, jnp.int32)
            key_sc[:, pl.ds(c0, CK)] = bits ^ ((bits >> 31) & 0x7FFFFFFF)
            return carry

        lax.fori_loop(0, n_ck, score_chunk, 0)

        def bit_step(i, u):
            bit = jnp.left_shift(jnp.int32(1), 31 - i)
            cand_u = u | bit
            cand = _rep_lanes(cand_u ^ INT_MIN, rep)

            def cnt_chunk(c, acc):
                c0 = pl.multiple_of(c * CK, CK)
                ge = (key_sc[:, pl.ds(c0, CK)] >= cand).astype(jnp.int32)
                part = ge[:, 0:LANES]
                for r in range(1, rep):
                    part = part + ge[:, LANES * r:LANES * (r + 1)]
                return acc + part

            acc = lax.fori_loop(0, n_ck, cnt_chunk, jnp.zeros((TQ, LANES), jnp.int32))
            cnt = jnp.sum(acc, axis=1, keepdims=True)
            return jnp.where(cnt >= topk, cand_u, u)

        u = lax.fori_loop(0, 32, bit_step, jnp.zeros((TQ, LANES), jnp.int32))
        thr_sc[...] = jnp.maximum(u ^ INT_MIN, KEY_NEG_INF + 1)

        m_sc[...] = jnp.full(m_sc.shape, -1e30, F32)
        l_sc[...] = jnp.zeros(l_sc.shape, F32)
        acc_sc[...] = jnp.zeros(acc_sc.shape, F32)

    def attend(s0, r0, bias_idx):
        mask = key_sc[:, pl.ds(s0, TQ)] >= _rep_lanes(thr_sc[...], TQ // LANES)
        for h in range(ATT_HEADS):
            hs = slice(ATT_HEAD_DIM * h, ATT_HEAD_DIM * (h + 1))
            s = lax.dot_general(q_ref[:, hs], k_ref[pl.ds(r0, TQ), hs],
                                (((1,), (1,)), ((), ())), preferred_element_type=F32) * scale
            if bias_idx is not None:
                s = s + tab_ref[bias_idx, h]
            s = jnp.where(mask, s, -jnp.inf)
            m_old = m_sc[h]
            m_new = jnp.maximum(m_old, jnp.max(s, axis=1, keepdims=True))
            alpha = jnp.exp(m_old - m_new)
            p = jnp.exp(s - m_new[:, 0:1])
            l_sc[h] = alpha * l_sc[h] + jnp.sum(p, axis=1, keepdims=True)
            pv = jnp.dot(p.astype(BF16), v_ref[pl.ds(r0, TQ), hs], preferred_element_type=F32)
            acc_sc[:, hs] = acc_sc[:, hs] * alpha[:, 0:1] + pv
            m_sc[h] = m_new

    @pl.when(kt <= last_kt)
    def _phase2():
        def sub(c, carry):
            r0 = pl.multiple_of(c * TQ, TQ)
            s0 = pl.multiple_of(kt * TK + r0, TQ)
            d = s0 - t0

            @pl.when(d == 0)
            def _():
                attend(s0, r0, 0)

            @pl.when(d == -TQ)
            def _():
                attend(s0, r0, 1)

            @pl.when(d < -TQ)
            def _():
                attend(s0, r0, None)

            return carry

        lax.fori_loop(0, n_sub, sub, 0)

    @pl.when(kt == last_kt)
    def _finish():
        for h in range(ATT_HEADS):
            hs = slice(ATT_HEAD_DIM * h, ATT_HEAD_DIM * (h + 1))
            o_ref[:, hs] = (acc_sc[:, hs] / l_sc[h][:, 0:1]).astype(o_ref.dtype)


def _sparse_attn(z, zs, kite, kito, tab, B, S, TQ, TK, CK):
    nq = S // TQ
    nk = S // TK
    topk = min(MAX_TOPK, S // 4)

    def row_blk(col):
        return lambda b, i, k: (b * nq + i, col)

    def kv_blk(col):
        def f(b, i, k):
            last = ((i + 1) * TQ - 1) // TK
            return (b * nk + jnp.minimum(k, last), col)
        return f

    kern = functools.partial(_attn_kernel, TQ=TQ, TK=TK, CK=CK, topk=topk)
    return pl.pallas_call(
        kern,
        grid=(B, nq, nk),
        in_specs=[
            pl.BlockSpec((TQ, ATT_WIDTH), row_blk(0)),
            pl.BlockSpec((TQ, ATT_WIDTH), row_blk(3)),
            pl.BlockSpec((TQ, LANES), lambda b, i, k: (b * nq + i, 0)),
            pl.BlockSpec((None, LANES, S), lambda b, i, k: (b, 0, 0), pipeline_mode=pl.Buffered(1)),
            pl.BlockSpec((None, LANES, S), lambda b, i, k: (b, 0, 0), pipeline_mode=pl.Buffered(1)),
            pl.BlockSpec((TK, ATT_WIDTH), kv_blk(1)),
            pl.BlockSpec((TK, ATT_WIDTH), kv_blk(2)),
            _resident((2, ATT_HEADS, TQ, TQ), lambda b, i, k: (0, 0, 0, 0)),
        ],
        out_specs=pl.BlockSpec((TQ, ATT_WIDTH), lambda b, i, k: (b * nq + i, 0)),
        out_shape=jax.ShapeDtypeStruct((B * S, ATT_WIDTH), BF16),
        scratch_shapes=[
            pltpu.VMEM((TQ, S), jnp.int32),
            pltpu.VMEM((IDX_HEADS, TQ, LANES), F32),
            pltpu.VMEM((TQ, LANES), jnp.int32),
            pltpu.VMEM((ATT_HEADS, TQ, LANES), F32),
            pltpu.VMEM((ATT_HEADS, TQ, LANES), F32),
            pltpu.VMEM((TQ, ATT_WIDTH), F32),
        ],
        compiler_params=_cparams(("parallel", "parallel", "arbitrary")),
        name="sparse_attn",
    )(z, z, zs, kite, kito, z, z, tab)


def _mix_kernel(x_ref, att_ref, gb_ref, gc_ref, u_ref, gch_ref, uh_ref, ga_ref, gv_ref,
                cw_ref, wa_ref, wb_ref, wo_ref, ling_ref, linb_ref, l1g_ref, l1b_ref,
                wr_ref, br_ref, h1_ref, h1b_ref, te_ref, tg_ref, *, tm, S, HALO, n_exp):
    i = pl.program_id(0)
    seq_start = (i * tm) % S == 0
    cu = gc_ref[...].astype(F32) * u_ref[...].astype(F32)
    cuh = gch_ref[...].astype(F32) * uh_ref[...].astype(F32)
    cuh = jnp.where(seq_start, 0.0, cuh)
    ext = jnp.concatenate([cuh, cu], axis=0)
    y = (cw_ref[2:3, :] * cu
         + cw_ref[1:2, :] * ext[HALO - 1:HALO - 1 + tm]
         + cw_ref[0:1, :] * ext[HALO - 2:HALO - 2 + tm])
    conv = (gb_ref[...].astype(F32) * y).astype(BF16)
    a = jnp.dot(att_ref[...], wa_ref[...], preferred_element_type=F32)
    c = jnp.dot(conv, wb_ref[...], preferred_element_type=F32)
    merged = (jax.nn.sigmoid(ga_ref[...].astype(F32)) * a
              + jax.nn.sigmoid(gv_ref[...].astype(F32)) * c).astype(BF16)
    mix = jnp.dot(merged, wo_ref[...], preferred_element_type=F32)
    h = _ln(x_ref[...], ling_ref[...], linb_ref[...])
    h1 = _ln(DEEPNORM_ALPHA * h + mix, l1g_ref[...], l1b_ref[...])
    h1_ref[...] = h1
    h1b_ref[...] = h1.astype(BF16)

    logits = jnp.dot(h1, wr_ref[...], preferred_element_type=F32,
                     precision=lax.Precision.HIGHEST) + br_ref[...]
    lane = lax.broadcasted_iota(jnp.int32, (tm, LANES), 1)
    work = jnp.where(lane < n_exp, logits, -jnp.inf)
    te = jnp.zeros((tm, LANES), jnp.int32)
    tv = jnp.zeros((tm, LANES), F32)
    v0 = None
    den = jnp.zeros((tm, 1), F32)
    for k in range(TOP_K_EXPERTS):
        vk = jnp.max(work, axis=1, keepdims=True)
        ik = jnp.min(jnp.where(work == vk, lane, LANES), axis=1, keepdims=True)
        if k == 0:
            v0 = vk
        ek = jnp.exp(vk - v0)
        den = den + ek
        te = jnp.where(lane == k, ik, te)
        tv = jnp.where(lane == k, ek, tv)
        work = jnp.where(lane == ik, -jnp.inf, work)
    te_ref[...] = te
    tg_ref[...] = tv / den


def _mix(x2, att, z, conv_w, wa, wb, wo, ling, linb, l1g, l1b, wr, br, S, tm, n_exp):
    M, D = x2.shape
    C = CONV_WIDTH
    HALO = 16
    hb = tm // HALO
    ga_blk = 4 * C // D
    gb_blk = (4 * C + 2 * D) // C
    kern = functools.partial(_mix_kernel, tm=tm, S=S, HALO=HALO, n_exp=n_exp)
    halo_map = lambda col: (lambda i: (jnp.maximum(i * hb - 1, 0), col))
    return pl.pallas_call(
        kern,
        grid=(M // tm,),
        in_specs=[
            pl.BlockSpec((tm, D), lambda i: (i, 0)),
            pl.BlockSpec((tm, ATT_WIDTH), lambda i: (i, 0)),
            pl.BlockSpec((tm, C), lambda i: (i, gb_blk)),
            pl.BlockSpec((tm, C), lambda i: (i, gb_blk + 1)),
            pl.BlockSpec((tm, C), lambda i: (i, gb_blk + 2)),
            pl.BlockSpec((HALO, C), halo_map(gb_blk + 1)),
            pl.BlockSpec((HALO, C), halo_map(gb_blk + 2)),
            pl.BlockSpec((tm, D), lambda i: (i, ga_blk)),
            pl.BlockSpec((tm, D), lambda i: (i, ga_blk + 1)),
            _resident((CONV_K, C), lambda i: (0, 0)),
            _resident((ATT_WIDTH, D), lambda i: (0, 0)),
            _resident((C, D), lambda i: (0, 0)),
            _resident((D, D), lambda i: (0, 0)),
            _resident((1, D), lambda i: (0, 0)),
            _resident((1, D), lambda i: (0, 0)),
            _resident((1, D), lambda i: (0, 0)),
            _resident((1, D), lambda i: (0, 0)),
            _resident((D, LANES), lambda i: (0, 0)),
            _resident((1, LANES), lambda i: (0, 0)),
        ],
        out_specs=[
            pl.BlockSpec((tm, D), lambda i: (i, 0)),
            pl.BlockSpec((tm, D), lambda i: (i, 0)),
            pl.BlockSpec((tm, LANES), lambda i: (i, 0)),
            pl.BlockSpec((tm, LANES), lambda i: (i, 0)),
        ],
        out_shape=[
            jax.ShapeDtypeStruct((M, D), F32),
            jax.ShapeDtypeStruct((M, D), BF16),
            jax.ShapeDtypeStruct((M, LANES), jnp.int32),
            jax.ShapeDtypeStruct((M, LANES), F32),
        ],
        compiler_params=_cparams(("parallel",)),
        name="mix_ln1",
    )(x2, att, z, z, z, z, z, z, z, conv_w, wa, wb, wo, ling, linb, l1g, l1b, wr, br)


def _moe_kernel(be_ref, xs_ref, rw_ref, wg_ref, bg_ref, wu_ref, bu_ref, wd_ref, bd_ref, o_ref, acc_sc):
    f = pl.program_id(1)

    @pl.when(f == 0)
    def _():
        acc_sc[...] = jnp.zeros(acc_sc.shape, F32)

    xb = xs_ref[...]
    g = jnp.dot(xb, wg_ref[...], preferred_element_type=F32) + bg_ref[...]
    u = jnp.dot(xb, wu_ref[...], preferred_element_type=F32) + bu_ref[...]
    g = jnp.minimum(g, SWIGLU_LIMIT)
    u = jnp.clip(u, -SWIGLU_LIMIT, SWIGLU_LIMIT)
    a = (u + 1.0) * (g * jax.nn.sigmoid(SWIGLU_ALPHA * g))
    acc_sc[...] += jnp.dot(a.astype(BF16), wd_ref[...], preferred_element_type=F32)

    @pl.when(f == pl.num_programs(1) - 1)
    def _():
        o_ref[...] = (acc_sc[...] + bd_ref[...]) * rw_ref[:, 0:1]


def _moe(block_e, xs, rw, wg, bg, wu, bu, wd, bd, tm, tf):
    R, D = xs.shape
    F = wg.shape[2]
    grid_spec = pltpu.PrefetchScalarGridSpec(
        num_scalar_prefetch=1,
        grid=(R // tm, F // tf),
        in_specs=[
            pl.BlockSpec((tm, D), lambda i, f, be: (i, 0)),
            pl.BlockSpec((tm, LANES), lambda i, f, be: (i, 0)),
            pl.BlockSpec((None, D, tf), lambda i, f, be: (be[i], 0, f)),
            pl.BlockSpec((None, 1, tf), lambda i, f, be: (be[i], 0, f)),
            pl.BlockSpec((None, D, tf), lambda i, f, be: (be[i], 0, f)),
            pl.BlockSpec((None, 1, tf), lambda i, f, be: (be[i], 0, f)),
            pl.BlockSpec((None, tf, D), lambda i, f, be: (be[i], f, 0)),
            pl.BlockSpec((None, 1, D), lambda i, f, be: (be[i], 0, 0)),
        ],
        out_specs=pl.BlockSpec((tm, D), lambda i, f, be: (i, 0)),
        scratch_shapes=[pltpu.VMEM((tm, D), F32)],
    )
    return pl.pallas_call(
        _moe_kernel,
        grid_spec=grid_spec,
        out_shape=jax.ShapeDtypeStruct((R, D), F32),
        compiler_params=_cparams(("parallel", "arbitrary")),
        name="moe_experts",
    )(block_e, xs, rw, wg, bg, wu, bu, wd, bd)


def _ple_kernel(h1_ref, h1b_ref, ff_ref, p_ref, wpg_ref, wpp_ref, g_ref, b_ref, o_ref):
    h1 = h1_ref[...]
    ff = ff_ref[0]
    for k in range(1, TOP_K_EXPERTS):
        ff = ff + ff_ref[k]
    gate = jax.nn.sigmoid(jnp.dot(h1b_ref[...], wpg_ref[...], preferred_element_type=F32))
    proj = jnp.dot(p_ref[...].astype(BF16), wpp_ref[...], preferred_element_type=F32)
    o_ref[...] = _ln(DEEPNORM_ALPHA * h1 + ff + gate * proj, g_ref[...], b_ref[...])


def _ple(h1, h1b, ffg, p2, wpg, wpp, g, b, tm):
    M, D = h1.shape
    P = p2.shape[1]
    return pl.pallas_call(
        _ple_kernel,
        grid=(M // tm,),
        in_specs=[
            pl.BlockSpec((tm, D), lambda i: (i, 0)),
            pl.BlockSpec((tm, D), lambda i: (i, 0)),
            pl.BlockSpec((TOP_K_EXPERTS, tm, D), lambda i: (0, i, 0)),
            pl.BlockSpec((tm, P), lambda i: (i, 0)),
            _resident((D, D), lambda i: (0, 0)),
            _resident((P, D), lambda i: (0, 0)),
            _resident((1, D), lambda i: (0, 0)),
            _resident((1, D), lambda i: (0, 0)),
        ],
        out_specs=pl.BlockSpec((tm, D), lambda i: (i, 0)),
        out_shape=jax.ShapeDtypeStruct((M, D), F32),
        compiler_params=_cparams(("parallel",)),
        name="ple_ln2",
    )(h1, h1b, ffg, p2, wpg, wpp, g, b)


def _pick(n, prefs):
    for t in prefs:
        if n % t == 0:
            return t
    return n


def kernel(x, p, ln_in_g, ln_in_b, rel_bias, w_in, conv_w, w_att_br, w_conv_br, w_o, ln1_g, ln1_b,
           w_router, b_router, w_gate, b_gate, w_up, b_up, w_down, b_down, w_ple_gate, w_ple_proj,
           ln2_g, ln2_b):
    B, S, D = x.shape
    M = B * S
    E = w_router.shape[-1]
    assert w_in.shape[0] == DEPTH == 1
    C = CONV_WIDTH
    x2 = x.reshape(M, D)
    row = lambda v: v.reshape(1, -1).astype(F32)

    w = w_in[0]
    sp = [ATT_WIDTH, ATT_WIDTH, ATT_WIDTH, IDX_HEADS * IDX_DIM, IDX_DIM, IDX_HEADS, C, C, C, D, D]
    offs = [0]
    for s_ in sp:
        offs.append(offs[-1] + s_)
    seg = lambda a, b_: w[:, offs[a]:offs[b_]]
    assert (2 * D) % C == 0 and ATT_WIDTH == C and IDX_HEADS * IDX_DIM == C
    w_main = jnp.concatenate([seg(0, 4), seg(9, 11), seg(6, 9)], axis=1)
    n_main = w_main.shape[1]
    tn = _pick(n_main, (1024, 512, 256, 128))
    w_main = w_main.astype(BF16)
    w_small = jnp.pad(seg(4, 6), ((0, 0), (0, LANES - IDX_DIM - IDX_HEADS))).astype(BF16)

    tm_in = _pick(M, (512, 256, 128))
    z, zs = _inproj(x2, row(ln_in_g), row(ln_in_b), w_main, w_small, tm_in, tn)

    TQ = _pick(S, (256, 128))
    TK = _pick(S, (1024, 512, 256))
    CK = _pick(S, (512, 256))
    ki_t = jnp.swapaxes(zs[:, :IDX_DIM].reshape(B, S, IDX_DIM), 1, 2).astype(BF16)
    zero = jnp.zeros_like(ki_t)
    kite = jnp.concatenate([ki_t, zero], axis=1)
    kito = jnp.concatenate([zero, ki_t], axis=1)
    tab = _bias_tab(rel_bias.astype(F32), TQ)
    att = _sparse_attn(z, zs, kite, kito, tab, B, S, TQ, TK, CK)

    wr = jnp.pad(w_router[0].astype(F32), ((0, 0), (0, LANES - E)))
    br = jnp.pad(b_router[0].astype(F32), (0, LANES - E)).reshape(1, LANES)
    tm_mix = _pick(M, (256, 128))
    h1, h1b, te, tg = _mix(x2, att, z, conv_w[0].astype(F32), w_att_br[0].astype(BF16),
                           w_conv_br[0].astype(BF16), w_o[0].astype(BF16),
                           row(ln_in_g), row(ln_in_b), row(ln1_g[0]), row(ln1_b[0]), wr, br,
                           S, tm_mix, E)

    top_e = te[:, :TOP_K_EXPERTS]
    gates = tg[:, :TOP_K_EXPERTS]
    n_assign = M * TOP_K_EXPERTS
    tm_e = _pick(n_assign, (512, 256, 128))
    e_flat = top_e.reshape(-1)
    order = jnp.argsort(e_flat, stable=True).astype(jnp.int32)
    onehot = (e_flat[:, None] == jnp.arange(E, dtype=jnp.int32)[None, :]).astype(jnp.int32)
    csum = jnp.cumsum(onehot, axis=0)
    counts = csum[-1]
    rank = jnp.take_along_axis(csum, e_flat[:, None], axis=1)[:, 0] - 1
    starts = jnp.cumsum(counts) - counts
    padded = (counts + tm_e - 1) // tm_e * tm_e
    pends = jnp.cumsum(padded)
    pstarts = pends - padded
    pos = pstarts[e_flat] + rank
    n_blocks = (n_assign + E * (tm_e - 1) + tm_e - 1) // tm_e
    n_rows = n_blocks * tm_e
    r = jnp.arange(n_rows, dtype=jnp.int32)
    row_e = jnp.minimum(jnp.searchsorted(pends, r, side='right'), E - 1).astype(jnp.int32)
    j = r - pstarts[row_e]
    valid = j < counts[row_e]
    src = jnp.clip(starts[row_e] + j, 0, n_assign - 1)
    a_id = order[src]
    row_tok = jnp.where(valid, a_id // TOP_K_EXPERTS, 0)
    row_w = jnp.where(valid, gates.reshape(-1)[a_id], 0.0)
    block_e = row_e.reshape(n_blocks, tm_e)[:, 0]

    xs = h1b[row_tok]
    rw = jnp.broadcast_to(row_w[:, None], (n_rows, LANES))
    F = w_gate.shape[-1]
    tf = _pick(F, (512, 256, 128))
    eo = _moe(block_e, xs, rw,
              w_gate[0].astype(BF16), b_gate[0].reshape(E, 1, F).astype(F32),
              w_up[0].astype(BF16), b_up[0].reshape(E, 1, F).astype(F32),
              w_down[0].astype(BF16), b_down[0].reshape(E, 1, D).astype(F32), tm_e, tf)

    ffg = eo[pos.reshape(M, TOP_K_EXPERTS).T]

    tm_p = _pick(M, (256, 128))
    out = _ple(h1, h1b, ffg, p[0].reshape(M, -1), w_ple_gate[0].astype(BF16),
               w_ple_proj[0].astype(BF16), row(ln2_g[0]), row(ln2_b[0]), tm_p)
    return out.reshape(B, S, D)
```

```python
import functools
import math

import jax
import jax.numpy as jnp
from jax import lax
from jax.experimental import pallas as pl
from jax.experimental.pallas import tpu as pltpu

CHUNK = 64
CHUNK_SHIFT = 6
assert 1 << CHUNK_SHIFT == CHUNK
ATT_HEADS = 8
ATT_HEAD_DIM = 128
ATT_WIDTH = ATT_HEADS * ATT_HEAD_DIM
IDX_HEADS = 16
IDX_DIM = 64
MAX_TOPK = 256
CONV_WIDTH = 1024
CONV_K = 3
REL_BUCKETS = 32
REL_MAX_DIST = 128
TOP_K_EXPERTS = 4
SWIGLU_LIMIT = 7.0
SWIGLU_ALPHA = 1.702
LN_EPS = 1e-5
DEPTH = 1
DEEPNORM_ALPHA = (2 * DEPTH) ** 0.25

LANES = 128
VMEM_LIMIT_BYTES = 56 * 1024 * 1024

LOG2E = math.log2(math.e)
INT_MIN = -(2 ** 31)
KEY_NEG_INF = (0xFF800000 ^ 0x7FFFFFFF) - (1 << 32)

F32 = jnp.float32
BF16 = jnp.bfloat16


def _cparams(sem):
    return pltpu.CompilerParams(dimension_semantics=sem, vmem_limit_bytes=VMEM_LIMIT_BYTES)


def _ln(xf, g, b):
    mu = jnp.mean(xf, axis=-1, keepdims=True)
    xc = xf - mu
    var = jnp.mean(xc * xc, axis=-1, keepdims=True)
    return xc * lax.rsqrt(var + LN_EPS) * g + b


def _rep_lanes(v, rep):
    return v if rep == 1 else jnp.concatenate([v] * rep, axis=1)


def _resident(shape, index_map):
    return pl.BlockSpec(shape, index_map, pipeline_mode=pl.Buffered(1))


def _bias_tab_kernel(rb_ref, o_ref, *, T):
    c = pl.program_id(0)
    h = pl.program_id(1)
    half = REL_BUCKETS // 2
    max_exact = half // 2
    j = lax.broadcasted_iota(jnp.int32, (T, T), 0)
    i = lax.broadcasted_iota(jnp.int32, (T, T), 1)
    rel = j - i - c * T
    ret = jnp.where(rel > 0, half, 0)
    n = jnp.abs(rel)
    nf = jnp.maximum(n, 1).astype(F32)
    large = max_exact + (jnp.log(nf / max_exact) / math.log(REL_MAX_DIST / max_exact)
                         * (half - max_exact)).astype(jnp.int32)
    large = jnp.minimum(large, half - 1)
    bucket = ret + jnp.where(n < max_exact, n, large)
    far = rb_ref[half - 1, h]
    acc = jnp.zeros((T, T), F32)
    for b in range(REL_BUCKETS):
        acc = jnp.where(bucket == b, rb_ref[b, h] - far, acc)
    o_ref[...] = acc * LOG2E


def _bias_tab(rel_bias, T):
    return pl.pallas_call(
        functools.partial(_bias_tab_kernel, T=T),
        grid=(2, ATT_HEADS),
        in_specs=[pl.BlockSpec(memory_space=pltpu.SMEM)],
        out_specs=pl.BlockSpec((None, None, T, T), lambda c, h: (c, h, 0, 0)),
        out_shape=jax.ShapeDtypeStruct((2, ATT_HEADS, T, T), F32),
        compiler_params=_cparams(("arbitrary", "arbitrary")),
        name="bias_tab",
    )(rel_bias)


def _inproj_kernel(x_ref, g_ref, b_ref, w_ref, ws_ref, z_ref, zs_ref, hn_sc):
    j = pl.program_id(1)

    @pl.when(j == 0)
    def _():
        hn = _ln(x_ref[...], g_ref[...], b_ref[...]).astype(BF16)
        hn_sc[...] = hn
        zs_ref[...] = jnp.dot(hn, ws_ref[...], preferred_element_type=F32)

    z_ref[...] = jnp.dot(hn_sc[...], w_ref[...], preferred_element_type=F32).astype(BF16)


def _inproj(x2, g, b, w_main, w_small, tm, tn):
    M, D = x2.shape
    N = w_main.shape[1]
    return pl.pallas_call(
        _inproj_kernel,
        grid=(M // tm, N // tn),
        in_specs=[
            pl.BlockSpec((tm, D), lambda i, j: (i, 0)),
            pl.BlockSpec((1, D), lambda i, j: (0, 0)),
            pl.BlockSpec((1, D), lambda i, j: (0, 0)),
            pl.BlockSpec((D, tn), lambda i, j: (0, j)),
            pl.BlockSpec((D, LANES), lambda i, j: (0, 0)),
        ],
        out_specs=[
            pl.BlockSpec((tm, tn), lambda i, j: (i, j)),
            pl.BlockSpec((tm, LANES), lambda i, j: (i, 0)),
        ],
        out_shape=[
            jax.ShapeDtypeStruct((M, N), BF16),
            jax.ShapeDtypeStruct((M, LANES), F32),
        ],
        scratch_shapes=[pltpu.VMEM((tm, D), BF16)],
        compiler_params=_cparams(("parallel", "arbitrary")),
        name="ln_inproj",
    )(x2, g, b, w_main, w_small)


def _attn_kernel(q_ref, qi_ref, em_ref, om_ref, wi_ref, kk_ref, k_ref, vt_ref, tab_ref, o_ref,
                 key_sc, qe_sc, qo_sc, thr_sc, m_sc, l_sc, acc_sc, madd_sc, x_sc, *, TQ, TK, CK, topk):
    qb = pl.program_id(1)
    kt = pl.program_id(2)
    t0 = qb * TQ
    lim_max = t0 + TQ
    last_kt = (lim_max - 1) // TK
    n_sub = TK // TQ
    nt = (((1,), (1,)), ((), ()))
    SLAB = 32

    @pl.when(kt == 0)
    def _phase1():
        qi = qi_ref[...]
        qe_sc[...] = qi * em_ref[...]
        qo_sc[...] = qi * om_ref[...]
        n_ck = (lim_max + CK - 1) // CK
        q_idx = t0 + lax.broadcasted_iota(jnp.int32, (1, TQ), 1)
        limit = ((q_idx >> CHUNK_SHIFT) + 1) << CHUNK_SHIFT

        def score_chunk(c, carry):
            c0 = pl.multiple_of(c * CK, CK)
            kk = kk_ref[pl.ds(c0, CK), :]
            acc = jnp.zeros((CK, TQ), F32)
            for jp in range(IDX_HEADS // 2):
                sl = slice(LANES * jp, LANES * (jp + 1))
                le = lax.dot_general(kk, qe_sc[:, sl], nt, preferred_element_type=F32)
                lo = lax.dot_general(kk, qo_sc[:, sl], nt, preferred_element_type=F32)
                acc = acc + jnp.maximum(le, 0.0) * wi_ref[2 * jp:2 * jp + 1, :]
                acc = acc + jnp.maximum(lo, 0.0) * wi_ref[2 * jp + 1:2 * jp + 2, :]
            s_idx = c0 + lax.broadcasted_iota(jnp.int32, (CK, TQ), 0)
            sc = jnp.where(s_idx < limit, acc, -jnp.inf)
            bits = pltpu.bitcast(sc, jnp.int32)
            key_sc[pl.ds(c0, CK), :] = bits ^ ((bits >> 31) & 0x7FFFFFFF)
            return carry

        lax.fori_loop(0, n_ck, score_chunk, 0)

        def bit_step(i, u):
            bit = jnp.left_shift(jnp.int32(1), 31 - i)
            cand_u = u | bit
            cand = jnp.broadcast_to(cand_u ^ INT_MIN, (SLAB, TQ))

            def cnt_chunk(c, acc):
                c0 = pl.multiple_of(c * CK, CK)
                for r in range(CK // SLAB):
                    blk = key_sc[pl.ds(c0 + SLAB * r, SLAB), :]
                    acc = acc + (blk >= cand).astype(jnp.int32)
                return acc

            acc = lax.fori_loop(0, n_ck, cnt_chunk, jnp.zeros((SLAB, TQ), jnp.int32))
            cnt = jnp.sum(acc, axis=0, keepdims=True)
            return jnp.where(cnt >= topk, cand_u, u)

        u = lax.fori_loop(0, 32, bit_step, jnp.zeros((1, TQ), jnp.int32))
        thr_sc[...] = jnp.broadcast_to(jnp.maximum(u ^ INT_MIN, KEY_NEG_INF + 1), thr_sc.shape)

        m_sc[...] = jnp.full(m_sc.shape, -1e30, F32)
        l_sc[...] = jnp.zeros(l_sc.shape, F32)
        acc_sc[...] = jnp.zeros(acc_sc.shape, F32)

    def attend(s0, r0, bias_idx):
        madd_sc[...] = jnp.where(key_sc[pl.ds(s0, TQ), :] >= thr_sc[0:1, :], 0.0, -jnp.inf)

        def logits(h):
            hs = slice(ATT_HEAD_DIM * h, ATT_HEAD_DIM * (h + 1))
            x = lax.dot_general(k_ref[pl.ds(r0, TQ), hs], q_ref[:, hs], nt,
                                preferred_element_type=F32) + madd_sc[...]
            if bias_idx is not None:
                x = x + tab_ref[bias_idx, h]
            x_sc[h % 2] = x
            return jnp.max(x, axis=0, keepdims=True)

        m_cur = logits(0)
        for h in range(ATT_HEADS):
            hs = slice(ATT_HEAD_DIM * h, ATT_HEAD_DIM * (h + 1))
            m_next = logits(h + 1) if h + 1 < ATT_HEADS else None
            m_old = m_sc[h:h + 1, :]
            m_new = jnp.maximum(m_old, m_cur)
            alpha = jnp.exp2(m_old - m_new)
            p = jnp.exp2(x_sc[h % 2] - m_new)
            l_sc[h:h + 1, :] = alpha * l_sc[h:h + 1, :] + jnp.sum(p, axis=0, keepdims=True)
            pv = jnp.dot(vt_ref[hs, pl.ds(r0, TQ)], p.astype(BF16), preferred_element_type=F32)
            acc_sc[hs, :] = acc_sc[hs, :] * alpha + pv
            m_sc[h:h + 1, :] = m_new
            m_cur = m_next

    @pl.when(kt <= last_kt)
    def _phase2():
        def sub(c, carry):
            r0 = pl.multiple_of(c * TQ, TQ)
            s0 = pl.multiple_of(kt * TK + r0, TQ)
            d = s0 - t0

            @pl.when(d == 0)
            def _():
                attend(s0, r0, 0)

            @pl.when(d == -TQ)
            def _():
                attend(s0, r0, 1)

            @pl.when(d < -TQ)
            def _():
                attend(s0, r0, None)

            return carry

        lax.fori_loop(0, n_sub, sub, 0)

    @pl.when(kt == last_kt)
    def _finish():
        for h in range(ATT_HEADS):
            hs = slice(ATT_HEAD_DIM * h, ATT_HEAD_DIM * (h + 1))
            o_ref[:, hs] = (acc_sc[hs, :] / l_sc[h:h + 1, :]).T.astype(o_ref.dtype)


def _sparse_attn(z, vt, kk, wit, tab, B, S, TQ, TK, CK):
    nq = S // TQ
    nk = S // TK
    topk = min(MAX_TOPK, S // 4)
    lane = jnp.arange(ATT_WIDTH, dtype=jnp.int32)[None, :]
    em = ((lane // IDX_DIM) % 2 == 0).astype(BF16)
    om = ((lane // IDX_DIM) % 2 == 1).astype(BF16)

    def row_blk(col):
        return lambda b, i, k: (b * nq + i, col)

    def last_vis(i, k):
        return jnp.minimum(k, ((i + 1) * TQ - 1) // TK)

    kern = functools.partial(_attn_kernel, TQ=TQ, TK=TK, CK=CK, topk=topk)
    return pl.pallas_call(
        kern,
        grid=(B, nq, nk),
        in_specs=[
            pl.BlockSpec((TQ, ATT_WIDTH), row_blk(0)),
            pl.BlockSpec((TQ, ATT_WIDTH), row_blk(3)),
            _resident((1, ATT_WIDTH), lambda b, i, k: (0, 0)),
            _resident((1, ATT_WIDTH), lambda b, i, k: (0, 0)),
            pl.BlockSpec((None, IDX_HEADS, TQ), lambda b, i, k: (b, 0, i)),
            pl.BlockSpec((S, LANES), lambda b, i, k: (b, 0), pipeline_mode=pl.Buffered(1)),
            pl.BlockSpec((TK, ATT_WIDTH), lambda b, i, k: (b * nk + last_vis(i, k), 1)),
            pl.BlockSpec((None, ATT_WIDTH, TK), lambda b, i, k: (b, 0, last_vis(i, k))),
            _resident((2, ATT_HEADS, TQ, TQ), lambda b, i, k: (0, 0, 0, 0)),
        ],
        out_specs=pl.BlockSpec((TQ, ATT_WIDTH), lambda b, i, k: (b * nq + i, 0)),
        out_shape=jax.ShapeDtypeStruct((B * S, ATT_WIDTH), BF16),
        scratch_shapes=[
            pltpu.VMEM((S, TQ), jnp.int32),
            pltpu.VMEM((TQ, ATT_WIDTH), BF16),
            pltpu.VMEM((TQ, ATT_WIDTH), BF16),
            pltpu.VMEM((8, TQ), jnp.int32),
            pltpu.VMEM((ATT_HEADS, TQ), F32),
            pltpu.VMEM((ATT_HEADS, TQ), F32),
            pltpu.VMEM((ATT_WIDTH, TQ), F32),
            pltpu.VMEM((TQ, TQ), F32),
            pltpu.VMEM((2, TQ, TQ), F32),
        ],
        compiler_params=_cparams(("parallel", "parallel", "arbitrary")),
        name="sparse_attn",
    )(z, z, em, om, wit, kk, z, vt, tab)


def _mix_kernel(x_ref, att_ref, gb_ref, gc_ref, u_ref, gch_ref, uh_ref, ga_ref, gv_ref,
                cw_ref, wa_ref, wb_ref, wo_ref, ling_ref, linb_ref, l1g_ref, l1b_ref,
                wr_ref, br_ref, h1_ref, h1b_ref, te_ref, tg_ref, *, tm, S, HALO, n_exp):
    i = pl.program_id(0)
    seq_start = (i * tm) % S == 0
    cu = gc_ref[...].astype(F32) * u_ref[...].astype(F32)
    cuh = gch_ref[...].astype(F32) * uh_ref[...].astype(F32)
    cuh = jnp.where(seq_start, 0.0, cuh)
    ext = jnp.concatenate([cuh, cu], axis=0)
    y = (cw_ref[2:3, :] * cu
         + cw_ref[1:2, :] * ext[HALO - 1:HALO - 1 + tm]
         + cw_ref[0:1, :] * ext[HALO - 2:HALO - 2 + tm])
    conv = (gb_ref[...].astype(F32) * y).astype(BF16)
    a = jnp.dot(att_ref[...], wa_ref[...], preferred_element_type=F32)
    c = jnp.dot(conv, wb_ref[...], preferred_element_type=F32)
    merged = (jax.nn.sigmoid(ga_ref[...].astype(F32)) * a
              + jax.nn.sigmoid(gv_ref[...].astype(F32)) * c).astype(BF16)
    mix = jnp.dot(merged, wo_ref[...], preferred_element_type=F32)
    h = _ln(x_ref[...], ling_ref[...], linb_ref[...])
    h1 = _ln(DEEPNORM_ALPHA * h + mix, l1g_ref[...], l1b_ref[...])
    h1_ref[...] = h1
    h1b_ref[...] = h1.astype(BF16)

    logits = jnp.dot(h1, wr_ref[...], preferred_element_type=F32,
                     precision=lax.Precision.HIGHEST) + br_ref[...]
    lane = lax.broadcasted_iota(jnp.int32, (tm, LANES), 1)
    work = jnp.where(lane < n_exp, logits, -jnp.inf)
    te = jnp.zeros((tm, LANES), jnp.int32)
    tv = jnp.zeros((tm, LANES), F32)
    v0 = None
    den = jnp.zeros((tm, 1), F32)
    for k in range(TOP_K_EXPERTS):
        vk = jnp.max(work, axis=1, keepdims=True)
        ik = jnp.min(jnp.where(work == vk, lane, LANES), axis=1, keepdims=True)
        if k == 0:
            v0 = vk
        ek = jnp.exp(vk - v0)
        den = den + ek
        te = jnp.where(lane == k, ik, te)
        tv = jnp.where(lane == k, ek, tv)
        work = jnp.where(lane == ik, -jnp.inf, work)
    te_ref[...] = te
    tg_ref[...] = tv / den


def _mix(x2, att, z, conv_w, wa, wb, wo, ling, linb, l1g, l1b, wr, br, S, tm, n_exp):
    M, D = x2.shape
    C = CONV_WIDTH
    HALO = 16
    hb = tm // HALO
    ga_blk = 4 * C // D
    gb_blk = (4 * C + 2 * D) // C
    kern = functools.partial(_mix_kernel, tm=tm, S=S, HALO=HALO, n_exp=n_exp)
    halo_map = lambda col: (lambda i: (jnp.maximum(i * hb - 1, 0), col))
    return pl.pallas_call(
        kern,
        grid=(M // tm,),
        in_specs=[
            pl.BlockSpec((tm, D), lambda i: (i, 0)),
            pl.BlockSpec((tm, ATT_WIDTH), lambda i: (i, 0)),
            pl.BlockSpec((tm, C), lambda i: (i, gb_blk)),
            pl.BlockSpec((tm, C), lambda i: (i, gb_blk + 1)),
            pl.BlockSpec((tm, C), lambda i: (i, gb_blk + 2)),
            pl.BlockSpec((HALO, C), halo_map(gb_blk + 1)),
            pl.BlockSpec((HALO, C), halo_map(gb_blk + 2)),
            pl.BlockSpec((tm, D), lambda i: (i, ga_blk)),
            pl.BlockSpec((tm, D), lambda i: (i, ga_blk + 1)),
            _resident((CONV_K, C), lambda i: (0, 0)),
            _resident((ATT_WIDTH, D), lambda i: (0, 0)),
            _resident((C, D), lambda i: (0, 0)),
            _resident((D, D), lambda i: (0, 0)),
            _resident((1, D), lambda i: (0, 0)),
            _resident((1, D), lambda i: (0, 0)),
            _resident((1, D), lambda i: (0, 0)),
            _resident((1, D), lambda i: (0, 0)),
            _resident((D, LANES), lambda i: (0, 0)),
            _resident((1, LANES), lambda i: (0, 0)),
        ],
        out_specs=[
            pl.BlockSpec((tm, D), lambda i: (i, 0)),
            pl.BlockSpec((tm, D), lambda i: (i, 0)),
            pl.BlockSpec((tm, LANES), lambda i: (i, 0)),
            pl.BlockSpec((tm, LANES), lambda i: (i, 0)),
        ],
        out_shape=[
            jax.ShapeDtypeStruct((M, D), F32),
            jax.ShapeDtypeStruct((M, D), BF16),
            jax.ShapeDtypeStruct((M, LANES), jnp.int32),
            jax.ShapeDtypeStruct((M, LANES), F32),
        ],
        compiler_params=_cparams(("parallel",)),
        name="mix_ln1",
    )(x2, att, z, z, z, z, z, z, z, conv_w, wa, wb, wo, ling, linb, l1g, l1b, wr, br)


def _moe_kernel(be_ref, xs_ref, rw_ref, wg_ref, bg_ref, wu_ref, bu_ref, wd_ref, bd_ref, o_ref, acc_sc):
    f = pl.program_id(1)

    @pl.when(f == 0)
    def _():
        acc_sc[...] = jnp.zeros(acc_sc.shape, F32)

    xb = xs_ref[...]
    g = jnp.dot(xb, wg_ref[...], preferred_element_type=F32) + bg_ref[...]
    u = jnp.dot(xb, wu_ref[...], preferred_element_type=F32) + bu_ref[...]
    g = jnp.minimum(g, SWIGLU_LIMIT)
    u = jnp.clip(u, -SWIGLU_LIMIT, SWIGLU_LIMIT)
    a = (u + 1.0) * (g * jax.nn.sigmoid(SWIGLU_ALPHA * g))
    acc_sc[...] += jnp.dot(a.astype(BF16), wd_ref[...], preferred_element_type=F32)

    @pl.when(f == pl.num_programs(1) - 1)
    def _():
        o_ref[...] = (acc_sc[...] + bd_ref[...]) * rw_ref[:, 0:1]


def _moe(block_e, xs, rw, wg, bg, wu, bu, wd, bd, tm, tf):
    R, D = xs.shape
    F = wg.shape[2]
    grid_spec = pltpu.PrefetchScalarGridSpec(
        num_scalar_prefetch=1,
        grid=(R // tm, F // tf),
        in_specs=[
            pl.BlockSpec((tm, D), lambda i, f, be: (i, 0)),
            pl.BlockSpec((tm, LANES), lambda i, f, be: (i, 0)),
            pl.BlockSpec((None, D, tf), lambda i, f, be: (be[i], 0, f)),
            pl.BlockSpec((None, 1, tf), lambda i, f, be: (be[i], 0, f)),
            pl.BlockSpec((None, D, tf), lambda i, f, be: (be[i], 0, f)),
            pl.BlockSpec((None, 1, tf), lambda i, f, be: (be[i], 0, f)),
            pl.BlockSpec((None, tf, D), lambda i, f, be: (be[i], f, 0)),
            pl.BlockSpec((None, 1, D), lambda i, f, be: (be[i], 0, 0)),
        ],
        out_specs=pl.BlockSpec((tm, D), lambda i, f, be: (i, 0)),
        scratch_shapes=[pltpu.VMEM((tm, D), F32)],
    )
    return pl.pallas_call(
        _moe_kernel,
        grid_spec=grid_spec,
        out_shape=jax.ShapeDtypeStruct((R, D), F32),
        compiler_params=_cparams(("parallel", "arbitrary")),
        name="moe_experts",
    )(block_e, xs, rw, wg, bg, wu, bu, wd, bd)


def _ple_kernel(h1_ref, h1b_ref, ff_ref, p_ref, wpg_ref, wpp_ref, g_ref, b_ref, o_ref):
    h1 = h1_ref[...]
    ff = ff_ref[0]
    for k in range(1, TOP_K_EXPERTS):
        ff = ff + ff_ref[k]
    gate = jax.nn.sigmoid(jnp.dot(h1b_ref[...], wpg_ref[...], preferred_element_type=F32))
    proj = jnp.dot(p_ref[...].astype(BF16), wpp_ref[...], preferred_element_type=F32)
    o_ref[...] = _ln(DEEPNORM_ALPHA * h1 + ff + gate * proj, g_ref[...], b_ref[...])


def _ple(h1, h1b, ffg, p2, wpg, wpp, g, b, tm):
    M, D = h1.shape
    P = p2.shape[1]
    return pl.pallas_call(
        _ple_kernel,
        grid=(M // tm,),
        in_specs=[
            pl.BlockSpec((tm, D), lambda i: (i, 0)),
            pl.BlockSpec((tm, D), lambda i: (i, 0)),
            pl.BlockSpec((TOP_K_EXPERTS, tm, D), lambda i: (0, i, 0)),
            pl.BlockSpec((tm, P), lambda i: (i, 0)),
            _resident((D, D), lambda i: (0, 0)),
            _resident((P, D), lambda i: (0, 0)),
            _resident((1, D), lambda i: (0, 0)),
            _resident((1, D), lambda i: (0, 0)),
        ],
        out_specs=pl.BlockSpec((tm, D), lambda i: (i, 0)),
        out_shape=jax.ShapeDtypeStruct((M, D), F32),
        compiler_params=_cparams(("parallel",)),
        name="ple_ln2",
    )(h1, h1b, ffg, p2, wpg, wpp, g, b)


def _pick(n, prefs):
    for t in prefs:
        if n % t == 0:
            return t
    return n


def kernel(x, p, ln_in_g, ln_in_b, rel_bias, w_in, conv_w, w_att_br, w_conv_br, w_o, ln1_g, ln1_b,
           w_router, b_router, w_gate, b_gate, w_up, b_up, w_down, b_down, w_ple_gate, w_ple_proj,
           ln2_g, ln2_b):
    B, S, D = x.shape
    M = B * S
    E = w_router.shape[-1]
    assert w_in.shape[0] == DEPTH == 1
    C = CONV_WIDTH
    x2 = x.reshape(M, D)
    row = lambda v: v.reshape(1, -1).astype(F32)

    w = w_in[0]
    sp = [ATT_WIDTH, ATT_WIDTH, ATT_WIDTH, IDX_HEADS * IDX_DIM, IDX_DIM, IDX_HEADS, C, C, C, D, D]
    offs = [0]
    for s_ in sp:
        offs.append(offs[-1] + s_)
    seg = lambda a, b_: w[:, offs[a]:offs[b_]]
    assert (2 * D) % C == 0 and ATT_WIDTH == C and IDX_HEADS * IDX_DIM == C
    w_q = seg(0, 1) * (ATT_HEAD_DIM ** -0.5 * LOG2E)
    w_main = jnp.concatenate([w_q, seg(1, 4), seg(9, 11), seg(6, 9)], axis=1)
    n_main = w_main.shape[1]
    tn = _pick(n_main, (1024, 512, 256, 128))
    w_main = w_main.astype(BF16)
    w_small = jnp.pad(seg(4, 6), ((0, 0), (0, LANES - IDX_DIM - IDX_HEADS))).astype(BF16)

    tm_in = _pick(M, (512, 256, 128))
    z, zs = _inproj(x2, row(ln_in_g), row(ln_in_b), w_main, w_small, tm_in, tn)

    TQ = _pick(S, (256, 128))
    TK = _pick(S, (1024, 512, 256))
    CK = _pick(S, (512, 256))
    ki = zs[:, :IDX_DIM].astype(BF16)
    kk = jnp.concatenate([ki, ki], axis=1)
    wit = jnp.swapaxes(zs[:, IDX_DIM:IDX_DIM + IDX_HEADS].reshape(B, S, IDX_HEADS), 1, 2)
    vt = jnp.swapaxes(z[:, 2 * ATT_WIDTH:3 * ATT_WIDTH].reshape(B, S, ATT_WIDTH), 1, 2)
    tab = _bias_tab(rel_bias.astype(F32), TQ)
    att = _sparse_attn(z, vt, kk, wit, tab, B, S, TQ, TK, CK)

    wr = jnp.pad(w_router[0].astype(F32), ((0, 0), (0, LANES - E)))
    br = jnp.pad(b_router[0].astype(F32), (0, LANES - E)).reshape(1, LANES)
    tm_mix = _pick(M, (256, 128))
    h1, h1b, te, tg = _mix(x2, att, z, conv_w[0].astype(F32), w_att_br[0].astype(BF16),
                           w_conv_br[0].astype(BF16), w_o[0].astype(BF16),
                           row(ln_in_g), row(ln_in_b), row(ln1_g[0]), row(ln1_b[0]), wr, br,
                           S, tm_mix, E)

    top_e = te[:, :TOP_K_EXPERTS]
    gates = tg[:, :TOP_K_EXPERTS]
    n_assign = M * TOP_K_EXPERTS
    tm_e = _pick(n_assign, (512, 256, 128))
    e_flat = top_e.reshape(-1)
    order = jnp.argsort(e_flat, stable=True).astype(jnp.int32)
    onehot = (e_flat[:, None] == jnp.arange(E, dtype=jnp.int32)[None, :]).astype(jnp.int32)
    csum = jnp.cumsum(onehot, axis=0)
    counts = csum[-1]
    rank = jnp.take_along_axis(csum, e_flat[:, None], axis=1)[:, 0] - 1
    starts = jnp.cumsum(counts) - counts
    padded = (counts + tm_e - 1) // tm_e * tm_e
    pends = jnp.cumsum(padded)
    pstarts = pends - padded
    pos = pstarts[e_flat] + rank
    n_blocks = (n_assign + E * (tm_e - 1) + tm_e - 1) // tm_e
    n_rows = n_blocks * tm_e
    r = jnp.arange(n_rows, dtype=jnp.int32)
    row_e = jnp.minimum(jnp.searchsorted(pends, r, side='right'), E - 1).astype(jnp.int32)
    j = r - pstarts[row_e]
    valid = j < counts[row_e]
    src = jnp.clip(starts[row_e] + j, 0, n_assign - 1)
    a_id = order[src]
    row_tok = jnp.where(valid, a_id // TOP_K_EXPERTS, 0)
    row_w = jnp.where(valid, gates.reshape(-1)[a_id], 0.0)
    block_e = row_e.reshape(n_blocks, tm_e)[:, 0]

    xs = h1b[row_tok]
    rw = jnp.broadcast_to(row_w[:, None], (n_rows, LANES))
    F = w_gate.shape[-1]
    tf = _pick(F, (512, 256, 128))
    eo = _moe(block_e, xs, rw,
              w_gate[0].astype(BF16), b_gate[0].reshape(E, 1, F).astype(F32),
              w_up[0].astype(BF16), b_up[0].reshape(E, 1, F).astype(F32),
              w_down[0].astype(BF16), b_down[0].reshape(E, 1, D).astype(F32), tm_e, tf)

    ffg = eo[pos.reshape(M, TOP_K_EXPERTS).T]

    tm_p = _pick(M, (256, 128))
    out = _ple(h1, h1b, ffg, p[0].reshape(M, -1), w_ple_gate[0].astype(BF16),
               w_ple_proj[0].astype(BF16), row(ln2_g[0]), row(ln2_b[0]), tm_p)
    return out.reshape(B, S, D)
```

```python
import functools
import math

import jax
import jax.numpy as jnp
from jax import lax
from jax.experimental import pallas as pl
from jax.experimental.pallas import tpu as pltpu

CHUNK = 64
CHUNK_SHIFT = 6
assert 1 << CHUNK_SHIFT == CHUNK
ATT_HEADS = 8
ATT_HEAD_DIM = 128
ATT_WIDTH = ATT_HEADS * ATT_HEAD_DIM
IDX_HEADS = 16
IDX_DIM = 64
MAX_TOPK = 256
CONV_WIDTH = 1024
CONV_K = 3
REL_BUCKETS = 32
REL_MAX_DIST = 128
TOP_K_EXPERTS = 4
SWIGLU_LIMIT = 7.0
SWIGLU_ALPHA = 1.702
LN_EPS = 1e-5
DEPTH = 1
DEEPNORM_ALPHA = (2 * DEPTH) ** 0.25

LANES = 128
MOE_SUB = 256
VMEM_LIMIT_BYTES = 56 * 1024 * 1024

LOG2E = math.log2(math.e)
INT_MIN = -(2 ** 31)
KEY_NEG_INF = (0xFF800000 ^ 0x7FFFFFFF) - (1 << 32)

F32 = jnp.float32
BF16 = jnp.bfloat16


def _cparams(sem):
    return pltpu.CompilerParams(dimension_semantics=sem, vmem_limit_bytes=VMEM_LIMIT_BYTES)


def _ln(xf, g, b):
    mu = jnp.mean(xf, axis=-1, keepdims=True)
    xc = xf - mu
    var = jnp.mean(xc * xc, axis=-1, keepdims=True)
    return xc * lax.rsqrt(var + LN_EPS) * g + b


def _rep_lanes(v, rep):
    return v if rep == 1 else jnp.concatenate([v] * rep, axis=1)


def _resident(shape, index_map):
    return pl.BlockSpec(shape, index_map, pipeline_mode=pl.Buffered(1))


def _bias_tab_kernel(rb_ref, o_ref, *, T):
    c = pl.program_id(0)
    h = pl.program_id(1)
    half = REL_BUCKETS // 2
    max_exact = half // 2
    j = lax.broadcasted_iota(jnp.int32, (T, T), 0)
    i = lax.broadcasted_iota(jnp.int32, (T, T), 1)
    rel = j - i - c * T
    ret = jnp.where(rel > 0, half, 0)
    n = jnp.abs(rel)
    nf = jnp.maximum(n, 1).astype(F32)
    large = max_exact + (jnp.log(nf / max_exact) / math.log(REL_MAX_DIST / max_exact)
                         * (half - max_exact)).astype(jnp.int32)
    large = jnp.minimum(large, half - 1)
    bucket = ret + jnp.where(n < max_exact, n, large)
    far = rb_ref[half - 1, h]
    acc = jnp.zeros((T, T), F32)
    for b in range(REL_BUCKETS):
        acc = jnp.where(bucket == b, rb_ref[b, h] - far, acc)
    o_ref[...] = acc * LOG2E


def _bias_tab(rel_bias, T):
    return pl.pallas_call(
        functools.partial(_bias_tab_kernel, T=T),
        grid=(2, ATT_HEADS),
        in_specs=[pl.BlockSpec(memory_space=pltpu.SMEM)],
        out_specs=pl.BlockSpec((None, None, T, T), lambda c, h: (c, h, 0, 0)),
        out_shape=jax.ShapeDtypeStruct((2, ATT_HEADS, T, T), F32),
        compiler_params=_cparams(("arbitrary", "arbitrary")),
        name="bias_tab",
    )(rel_bias)


def _inproj_kernel(x_ref, g_ref, b_ref, w_ref, ws_ref, z_ref, zs_ref, hn_sc):
    j = pl.program_id(1)

    @pl.when(j == 0)
    def _():
        hn = _ln(x_ref[...], g_ref[...], b_ref[...]).astype(BF16)
        hn_sc[...] = hn
        zs_ref[...] = jnp.dot(hn, ws_ref[...], preferred_element_type=F32)

    z_ref[...] = jnp.dot(hn_sc[...], w_ref[...], preferred_element_type=F32).astype(BF16)


def _inproj(x2, g, b, w_main, w_small, tm, tn):
    M, D = x2.shape
    N = w_main.shape[1]
    return pl.pallas_call(
        _inproj_kernel,
        grid=(M // tm, N // tn),
        in_specs=[
            pl.BlockSpec((tm, D), lambda i, j: (i, 0)),
            pl.BlockSpec((1, D), lambda i, j: (0, 0)),
            pl.BlockSpec((1, D), lambda i, j: (0, 0)),
            pl.BlockSpec((D, tn), lambda i, j: (0, j)),
            pl.BlockSpec((D, LANES), lambda i, j: (0, 0)),
        ],
        out_specs=[
            pl.BlockSpec((tm, tn), lambda i, j: (i, j)),
            pl.BlockSpec((tm, LANES), lambda i, j: (i, 0)),
        ],
        out_shape=[
            jax.ShapeDtypeStruct((M, N), BF16),
            jax.ShapeDtypeStruct((M, LANES), F32),
        ],
        scratch_shapes=[pltpu.VMEM((tm, D), BF16)],
        compiler_params=_cparams(("parallel", "arbitrary")),
        name="ln_inproj",
    )(x2, g, b, w_main, w_small)


def _attn_kernel(qt_ref, qit_ref, em_ref, om_ref, wi_ref, kk_ref, k_ref, vt_ref, tab_ref, o_ref,
                 key_sc, qe_sc, qo_sc, thr_sc, m_sc, l_sc, acc_sc, madd_sc, x_sc, p_sc,
                 *, TQ, TK, CK, topk):
    qb = pl.program_id(1)
    kt = pl.program_id(2)
    t0 = qb * TQ
    lim_max = t0 + TQ
    last_kt = (lim_max - 1) // TK
    n_sub = TK // TQ
    SLAB = 32

    @pl.when(kt == 0)
    def _phase1():
        qi = qit_ref[...]
        qe_sc[...] = qi * em_ref[...]
        qo_sc[...] = qi * om_ref[...]
        n_ck = (lim_max + CK - 1) // CK
        q_idx = t0 + lax.broadcasted_iota(jnp.int32, (1, TQ), 1)
        limit = ((q_idx >> CHUNK_SHIFT) + 1) << CHUNK_SHIFT

        def score_chunk(c, carry):
            c0 = pl.multiple_of(c * CK, CK)
            kk = kk_ref[pl.ds(c0, CK), :]
            acc = jnp.zeros((CK, TQ), F32)
            for jp in range(IDX_HEADS // 2):
                sl = slice(LANES * jp, LANES * (jp + 1))
                le = jnp.dot(kk, qe_sc[sl, :], preferred_element_type=F32)
                lo = jnp.dot(kk, qo_sc[sl, :], preferred_element_type=F32)
                acc = acc + jnp.maximum(le, 0.0) * wi_ref[2 * jp:2 * jp + 1, :]
                acc = acc + jnp.maximum(lo, 0.0) * wi_ref[2 * jp + 1:2 * jp + 2, :]
            s_idx = c0 + lax.broadcasted_iota(jnp.int32, (CK, TQ), 0)
            sc = jnp.where(s_idx < limit, acc, -jnp.inf)
            bits = pltpu.bitcast(sc, jnp.int32)
            key_sc[pl.ds(c0, CK), :] = bits ^ ((bits >> 31) & 0x7FFFFFFF)
            return carry

        lax.fori_loop(0, n_ck, score_chunk, 0)

        def bit_step(state):
            i, u, settled, _ = state
            bit = jnp.left_shift(jnp.int32(1), 31 - i)
            cand_u = u | bit
            cand = jnp.broadcast_to(cand_u ^ INT_MIN, (SLAB, TQ))

            def cnt_chunk(c, acc):
                c0 = pl.multiple_of(c * CK, CK)
                for r in range(CK // SLAB):
                    blk = key_sc[pl.ds(c0 + SLAB * r, SLAB), :]
                    acc = acc + (blk >= cand).astype(jnp.int32)
                return acc

            acc = lax.fori_loop(0, n_ck, cnt_chunk, jnp.zeros((SLAB, TQ), jnp.int32))
            cnt = jnp.sum(acc, axis=0, keepdims=True)
            settled = jnp.where(cnt == topk, 1, settled)
            return i + 1, jnp.where(cnt >= topk, cand_u, u), settled, jnp.min(settled)

        settled0 = (limit <= topk).astype(jnp.int32)
        state = (jnp.int32(0), jnp.zeros((1, TQ), jnp.int32), settled0, jnp.min(settled0))
        _, u, _, _ = lax.while_loop(lambda s: (s[0] < 32) & (s[3] == 0), bit_step, state)
        thr_sc[...] = jnp.broadcast_to(jnp.maximum(u ^ INT_MIN, KEY_NEG_INF + 1), thr_sc.shape)

        m_sc[...] = jnp.full(m_sc.shape, -1e30, F32)
        l_sc[...] = jnp.zeros(l_sc.shape, F32)
        acc_sc[...] = jnp.zeros(acc_sc.shape, F32)

    def attend(s0, r0, bias_idx):
        madd_sc[...] = jnp.where(key_sc[pl.ds(s0, TQ), :] >= thr_sc[0:1, :], 0.0, -jnp.inf)

        def logits(h):
            hs = slice(ATT_HEAD_DIM * h, ATT_HEAD_DIM * (h + 1))
            x = jnp.dot(k_ref[pl.ds(r0, TQ), hs], qt_ref[hs, :],
                        preferred_element_type=F32) + madd_sc[...]
            if bias_idx is not None:
                x = x + tab_ref[bias_idx, h]
            x_sc[h % 3] = x
            return jnp.max(x, axis=0, keepdims=True)

        def weighted_values(h, alpha):
            hs = slice(ATT_HEAD_DIM * h, ATT_HEAD_DIM * (h + 1))
            pv = jnp.dot(vt_ref[hs, pl.ds(r0, TQ)], p_sc[h % 2], preferred_element_type=F32)
            acc_sc[hs, :] = acc_sc[hs, :] * alpha + pv

        m_cur = {0: logits(0), 1: logits(1)}
        alphas = {}
        for h in range(ATT_HEADS):
            if h + 2 < ATT_HEADS:
                m_cur[h + 2] = logits(h + 2)
            if h >= 1:
                weighted_values(h - 1, alphas.pop(h - 1))
            m_old = m_sc[h:h + 1, :]
            m_new = jnp.maximum(m_old, m_cur.pop(h))
            alpha = jnp.exp2(m_old - m_new)
            p = jnp.exp2(x_sc[h % 3] - m_new)
            l_sc[h:h + 1, :] = alpha * l_sc[h:h + 1, :] + jnp.sum(p, axis=0, keepdims=True)
            p_sc[h % 2] = p.astype(BF16)
            m_sc[h:h + 1, :] = m_new
            alphas[h] = alpha
        weighted_values(ATT_HEADS - 1, alphas.pop(ATT_HEADS - 1))

    @pl.when(kt <= last_kt)
    def _phase2():
        def sub(c, carry):
            r0 = pl.multiple_of(c * TQ, TQ)
            s0 = pl.multiple_of(kt * TK + r0, TQ)
            d = s0 - t0

            @pl.when(d == 0)
            def _():
                attend(s0, r0, 0)

            @pl.when(d == -TQ)
            def _():
                attend(s0, r0, 1)

            @pl.when(d < -TQ)
            def _():
                attend(s0, r0, None)

            return carry

        lax.fori_loop(0, n_sub, sub, 0)

    @pl.when(kt == last_kt)
    def _finish():
        for h in range(ATT_HEADS):
            hs = slice(ATT_HEAD_DIM * h, ATT_HEAD_DIM * (h + 1))
            o_ref[:, hs] = (acc_sc[hs, :] / l_sc[h:h + 1, :]).T.astype(o_ref.dtype)


def _sparse_attn(z, qt, qit, vt, kk, wit, tab, B, S, TQ, TK, CK):
    nq = S // TQ
    nk = S // TK
    topk = min(MAX_TOPK, S // 4)
    feat = jnp.arange(ATT_WIDTH, dtype=jnp.int32)[:, None]
    em = jnp.broadcast_to(((feat // IDX_DIM) % 2 == 0).astype(BF16), (ATT_WIDTH, TQ))
    om = jnp.broadcast_to(((feat // IDX_DIM) % 2 == 1).astype(BF16), (ATT_WIDTH, TQ))

    def last_vis(i, k):
        return jnp.minimum(k, ((i + 1) * TQ - 1) // TK)

    kern = functools.partial(_attn_kernel, TQ=TQ, TK=TK, CK=CK, topk=topk)
    return pl.pallas_call(
        kern,
        grid=(B, nq, nk),
        in_specs=[
            pl.BlockSpec((None, ATT_WIDTH, TQ), lambda b, i, k: (b, 0, i)),
            pl.BlockSpec((None, ATT_WIDTH, TQ), lambda b, i, k: (b, 0, i)),
            _resident((ATT_WIDTH, TQ), lambda b, i, k: (0, 0)),
            _resident((ATT_WIDTH, TQ), lambda b, i, k: (0, 0)),
            pl.BlockSpec((None, IDX_HEADS, TQ), lambda b, i, k: (b, 0, i)),
            pl.BlockSpec((S, LANES), lambda b, i, k: (b, 0), pipeline_mode=pl.Buffered(1)),
            pl.BlockSpec((TK, ATT_WIDTH), lambda b, i, k: (b * nk + last_vis(i, k), 1)),
            pl.BlockSpec((None, ATT_WIDTH, TK), lambda b, i, k: (b, 0, last_vis(i, k))),
            _resident((2, ATT_HEADS, TQ, TQ), lambda b, i, k: (0, 0, 0, 0)),
        ],
        out_specs=pl.BlockSpec((TQ, ATT_WIDTH), lambda b, i, k: (b * nq + i, 0)),
        out_shape=jax.ShapeDtypeStruct((B * S, ATT_WIDTH), BF16),
        scratch_shapes=[
            pltpu.VMEM((S, TQ), jnp.int32),
            pltpu.VMEM((ATT_WIDTH, TQ), BF16),
            pltpu.VMEM((ATT_WIDTH, TQ), BF16),
            pltpu.VMEM((8, TQ), jnp.int32),
            pltpu.VMEM((ATT_HEADS, TQ), F32),
            pltpu.VMEM((ATT_HEADS, TQ), F32),
            pltpu.VMEM((ATT_WIDTH, TQ), F32),
            pltpu.VMEM((TQ, TQ), F32),
            pltpu.VMEM((3, TQ, TQ), F32),
            pltpu.VMEM((2, TQ, TQ), BF16),
        ],
        compiler_params=_cparams(("parallel", "parallel", "arbitrary")),
        name="sparse_attn",
    )(qt, qit, em, om, wit, kk, z, vt, tab)


def _mix_kernel(x_ref, att_ref, gb_ref, gc_ref, u_ref, gch_ref, uh_ref, ga_ref, gv_ref,
                cw_ref, wa_ref, wb_ref, wo_ref, ling_ref, linb_ref, l1g_ref, l1b_ref,
                wrh_ref, wrl_ref, br_ref, h1_ref, h1b_ref, te_ref, tg_ref, *, tm, S, HALO, n_exp):
    i = pl.program_id(0)
    seq_start = (i * tm) % S == 0
    cu = gc_ref[...].astype(F32) * u_ref[...].astype(F32)
    cuh = gch_ref[...].astype(F32) * uh_ref[...].astype(F32)
    cuh = jnp.where(seq_start, 0.0, cuh)
    ext = jnp.concatenate([cuh, cu], axis=0)
    y = (cw_ref[2:3, :] * cu
         + cw_ref[1:2, :] * ext[HALO - 1:HALO - 1 + tm]
         + cw_ref[0:1, :] * ext[HALO - 2:HALO - 2 + tm])
    conv = (gb_ref[...].astype(F32) * y).astype(BF16)
    a = jnp.dot(att_ref[...], wa_ref[...], preferred_element_type=F32)
    c = jnp.dot(conv, wb_ref[...], preferred_element_type=F32)
    merged = (jax.nn.sigmoid(ga_ref[...].astype(F32)) * a
              + jax.nn.sigmoid(gv_ref[...].astype(F32)) * c).astype(BF16)
    mix = jnp.dot(merged, wo_ref[...], preferred_element_type=F32)
    h = _ln(x_ref[...], ling_ref[...], linb_ref[...])
    h1 = _ln(DEEPNORM_ALPHA * h + mix, l1g_ref[...], l1b_ref[...])
    h1_ref[...] = h1
    h1b_ref[...] = h1.astype(BF16)

    h1_hi = h1.astype(BF16)
    h1_lo = (h1 - h1_hi.astype(F32)).astype(BF16)
    logits = (jnp.dot(h1_hi, wrh_ref[...], preferred_element_type=F32)
              + (jnp.dot(h1_hi, wrl_ref[...], preferred_element_type=F32)
                 + jnp.dot(h1_lo, wrh_ref[...], preferred_element_type=F32))) + br_ref[...]
    lane = lax.broadcasted_iota(jnp.int32, (tm, LANES), 1)
    work = jnp.where(lane < n_exp, logits, -jnp.inf)
    te = jnp.zeros((tm, LANES), jnp.int32)
    tv = jnp.zeros((tm, LANES), F32)
    v0 = None
    den = jnp.zeros((tm, 1), F32)
    for k in range(TOP_K_EXPERTS):
        vk = jnp.max(work, axis=1, keepdims=True)
        ik = jnp.min(jnp.where(work == vk, lane, LANES), axis=1, keepdims=True)
        if k == 0:
            v0 = vk
        ek = jnp.exp(vk - v0)
        den = den + ek
        te = jnp.where(lane == k, ik, te)
        tv = jnp.where(lane == k, ek, tv)
        work = jnp.where(lane == ik, -jnp.inf, work)
    te_ref[...] = te
    tg_ref[...] = tv / den


def _mix(x2, att, z, conv_w, wa, wb, wo, ling, linb, l1g, l1b, wr_hi, wr_lo, br, S, tm, n_exp):
    M, D = x2.shape
    C = CONV_WIDTH
    HALO = 16
    hb = tm // HALO
    ga_blk = 4 * C // D
    gb_blk = (4 * C + 2 * D) // C
    kern = functools.partial(_mix_kernel, tm=tm, S=S, HALO=HALO, n_exp=n_exp)
    halo_map = lambda col: (lambda i: (jnp.maximum(i * hb - 1, 0), col))
    return pl.pallas_call(
        kern,
        grid=(M // tm,),
        in_specs=[
            pl.BlockSpec((tm, D), lambda i: (i, 0)),
            pl.BlockSpec((tm, ATT_WIDTH), lambda i: (i, 0)),
            pl.BlockSpec((tm, C), lambda i: (i, gb_blk)),
            pl.BlockSpec((tm, C), lambda i: (i, gb_blk + 1)),
            pl.BlockSpec((tm, C), lambda i: (i, gb_blk + 2)),
            pl.BlockSpec((HALO, C), halo_map(gb_blk + 1)),
            pl.BlockSpec((HALO, C), halo_map(gb_blk + 2)),
            pl.BlockSpec((tm, D), lambda i: (i, ga_blk)),
            pl.BlockSpec((tm, D), lambda i: (i, ga_blk + 1)),
            _resident((CONV_K, C), lambda i: (0, 0)),
            _resident((ATT_WIDTH, D), lambda i: (0, 0)),
            _resident((C, D), lambda i: (0, 0)),
            _resident((D, D), lambda i: (0, 0)),
            _resident((1, D), lambda i: (0, 0)),
            _resident((1, D), lambda i: (0, 0)),
            _resident((1, D), lambda i: (0, 0)),
            _resident((1, D), lambda i: (0, 0)),
            _resident((D, LANES), lambda i: (0, 0)),
            _resident((D, LANES), lambda i: (0, 0)),
            _resident((1, LANES), lambda i: (0, 0)),
        ],
        out_specs=[
            pl.BlockSpec((tm, D), lambda i: (i, 0)),
            pl.BlockSpec((tm, D), lambda i: (i, 0)),
            pl.BlockSpec((tm, LANES), lambda i: (i, 0)),
            pl.BlockSpec((tm, LANES), lambda i: (i, 0)),
        ],
        out_shape=[
            jax.ShapeDtypeStruct((M, D), F32),
            jax.ShapeDtypeStruct((M, D), BF16),
            jax.ShapeDtypeStruct((M, LANES), jnp.int32),
            jax.ShapeDtypeStruct((M, LANES), F32),
        ],
        compiler_params=_cparams(("parallel",)),
        name="mix_ln1",
    )(x2, att, z, z, z, z, z, z, z, conv_w, wa, wb, wo, ling, linb, l1g, l1b, wr_hi, wr_lo, br)


def _moe_kernel(be_ref, bv_ref, xs_ref, rw_ref, wg_ref, bg_ref, wu_ref, bu_ref, wd_ref, bd_ref, o_ref, acc_sc):
    i = pl.program_id(0)
    f = pl.program_id(1)
    last_f = pl.num_programs(1) - 1
    used = bv_ref[i] > 0

    @pl.when(used)
    def _():
        @pl.when(f == 0)
        def _():
            acc_sc[...] = jnp.zeros(acc_sc.shape, F32)

        xb = xs_ref[...]
        tf = wg_ref.shape[1]
        n_sub = max(1, tf // MOE_SUB)
        sub = tf // n_sub

        def gate_up(c):
            cs = slice(sub * c, sub * (c + 1))
            g = jnp.dot(xb, wg_ref[:, cs], preferred_element_type=F32) + bg_ref[:, cs]
            u = jnp.dot(xb, wu_ref[:, cs], preferred_element_type=F32) + bu_ref[:, cs]
            return g, u

        def act_down(gu, c):
            g, u = gu
            g = jnp.minimum(g, SWIGLU_LIMIT)
            u = jnp.clip(u, -SWIGLU_LIMIT, SWIGLU_LIMIT)
            a = (u + 1.0) * (g * jax.nn.sigmoid(SWIGLU_ALPHA * g))
            return jnp.dot(a.astype(BF16), wd_ref[sub * c:sub * (c + 1), :], preferred_element_type=F32)

        pending = gate_up(0)
        d = None
        for c in range(n_sub):
            nxt = gate_up(c + 1) if c + 1 < n_sub else None
            dc = act_down(pending, c)
            d = dc if d is None else d + dc
            pending = nxt
        acc_sc[...] += d

        @pl.when(f == last_f)
        def _():
            o_ref[...] = ((acc_sc[...] + bd_ref[...]) * rw_ref[:, 0:1]).astype(o_ref.dtype)

    @pl.when(jnp.logical_not(used) & (f == last_f))
    def _():
        o_ref[...] = jnp.zeros(o_ref.shape, o_ref.dtype)


def _cast_kernel(x_ref, o_ref):
    o_ref[...] = x_ref[...].astype(o_ref.dtype)


def _cast_bf16(w):
    E, A, Bc = w.shape
    ta = _pick(A, (1024, 512, 256, 128))
    return pl.pallas_call(
        _cast_kernel,
        grid=(E, A // ta),
        in_specs=[pl.BlockSpec((None, ta, Bc), lambda e, a: (e, a, 0))],
        out_specs=pl.BlockSpec((None, ta, Bc), lambda e, a: (e, a, 0)),
        out_shape=jax.ShapeDtypeStruct(w.shape, BF16),
        compiler_params=_cparams(("parallel", "parallel")),
        name="cast_bf16",
    )(w)


def _moe(block_e, block_used, xs, rw, wg, bg, wu, bu, wd, bd, tm, tf):
    R, D = xs.shape
    F = wg.shape[2]
    nf = F // tf

    def f_blk(i, f, bv):
        return jnp.where(bv[i] > 0, f, nf - 1)

    grid_spec = pltpu.PrefetchScalarGridSpec(
        num_scalar_prefetch=2,
        grid=(R // tm, nf),
        in_specs=[
            pl.BlockSpec((tm, D), lambda i, f, be, bv: (i, 0)),
            pl.BlockSpec((tm, LANES), lambda i, f, be, bv: (i, 0)),
            pl.BlockSpec((None, D, tf), lambda i, f, be, bv: (be[i], 0, f_blk(i, f, bv))),
            pl.BlockSpec((None, 1, tf), lambda i, f, be, bv: (be[i], 0, f_blk(i, f, bv))),
            pl.BlockSpec((None, D, tf), lambda i, f, be, bv: (be[i], 0, f_blk(i, f, bv))),
            pl.BlockSpec((None, 1, tf), lambda i, f, be, bv: (be[i], 0, f_blk(i, f, bv))),
            pl.BlockSpec((None, tf, D), lambda i, f, be, bv: (be[i], f_blk(i, f, bv), 0)),
            pl.BlockSpec((None, 1, D), lambda i, f, be, bv: (be[i], 0, 0)),
        ],
        out_specs=pl.BlockSpec((tm, D), lambda i, f, be, bv: (i, 0)),
        scratch_shapes=[pltpu.VMEM((tm, D), F32)],
    )
    return pl.pallas_call(
        _moe_kernel,
        grid_spec=grid_spec,
        out_shape=jax.ShapeDtypeStruct((R, D), BF16),
        compiler_params=_cparams(("parallel", "arbitrary")),
        name="moe_experts",
    )(block_e, block_used, xs, rw, wg, bg, wu, bu, wd, bd)


def _ple_kernel(h1_ref, h1b_ref, ff_ref, p_ref, wpg_ref, wpp_ref, g_ref, b_ref, o_ref):
    h1 = h1_ref[...]
    ff = ff_ref[0].astype(F32)
    for k in range(1, TOP_K_EXPERTS):
        ff = ff + ff_ref[k].astype(F32)
    gate = jax.nn.sigmoid(jnp.dot(h1b_ref[...], wpg_ref[...], preferred_element_type=F32))
    proj = jnp.dot(p_ref[...].astype(BF16), wpp_ref[...], preferred_element_type=F32)
    o_ref[...] = _ln(DEEPNORM_ALPHA * h1 + ff + gate * proj, g_ref[...], b_ref[...])


def _ple(h1, h1b, ffg, p2, wpg, wpp, g, b, tm):
    M, D = h1.shape
    P = p2.shape[1]
    return pl.pallas_call(
        _ple_kernel,
        grid=(M // tm,),
        in_specs=[
            pl.BlockSpec((tm, D), lambda i: (i, 0)),
            pl.BlockSpec((tm, D), lambda i: (i, 0)),
            pl.BlockSpec((TOP_K_EXPERTS, tm, D), lambda i: (0, i, 0)),
            pl.BlockSpec((tm, P), lambda i: (i, 0)),
            _resident((D, D), lambda i: (0, 0)),
            _resident((P, D), lambda i: (0, 0)),
            _resident((1, D), lambda i: (0, 0)),
            _resident((1, D), lambda i: (0, 0)),
        ],
        out_specs=pl.BlockSpec((tm, D), lambda i: (i, 0)),
        out_shape=jax.ShapeDtypeStruct((M, D), F32),
        compiler_params=_cparams(("parallel",)),
        name="ple_ln2",
    )(h1, h1b, ffg, p2, wpg, wpp, g, b)


def _pick(n, prefs):
    for t in prefs:
        if n % t == 0:
            return t
    return n


def kernel(x, p, ln_in_g, ln_in_b, rel_bias, w_in, conv_w, w_att_br, w_conv_br, w_o, ln1_g, ln1_b,
           w_router, b_router, w_gate, b_gate, w_up, b_up, w_down, b_down, w_ple_gate, w_ple_proj,
           ln2_g, ln2_b):
    B, S, D = x.shape
    M = B * S
    E = w_router.shape[-1]
    assert w_in.shape[0] == DEPTH == 1
    C = CONV_WIDTH
    x2 = x.reshape(M, D)
    row = lambda v: v.reshape(1, -1).astype(F32)

    w = w_in[0]
    sp = [ATT_WIDTH, ATT_WIDTH, ATT_WIDTH, IDX_HEADS * IDX_DIM, IDX_DIM, IDX_HEADS, C, C, C, D, D]
    offs = [0]
    for s_ in sp:
        offs.append(offs[-1] + s_)
    seg = lambda a, b_: w[:, offs[a]:offs[b_]]
    assert (2 * D) % C == 0 and ATT_WIDTH == C and IDX_HEADS * IDX_DIM == C
    w_q = seg(0, 1) * (ATT_HEAD_DIM ** -0.5 * LOG2E)
    w_main = jnp.concatenate([w_q, seg(1, 4), seg(9, 11), seg(6, 9)], axis=1)
    n_main = w_main.shape[1]
    tn = _pick(n_main, (1024, 512, 256, 128))
    w_main = w_main.astype(BF16)
    w_small = jnp.pad(seg(4, 6), ((0, 0), (0, LANES - IDX_DIM - IDX_HEADS))).astype(BF16)

    tm_in = _pick(M, (512, 256, 128))
    z, zs = _inproj(x2, row(ln_in_g), row(ln_in_b), w_main, w_small, tm_in, tn)

    TQ = _pick(S, (256, 128))
    TK = _pick(S, (1024, 512, 256))
    CK = _pick(S, (512, 256))
    ki = zs[:, :IDX_DIM].astype(BF16)
    kk = jnp.concatenate([ki, ki], axis=1)
    wit = jnp.swapaxes(zs[:, IDX_DIM:IDX_DIM + IDX_HEADS].reshape(B, S, IDX_HEADS), 1, 2)
    col_t = lambda c: jnp.swapaxes(z[:, c * ATT_WIDTH:(c + 1) * ATT_WIDTH].reshape(B, S, ATT_WIDTH), 1, 2)
    qt, vt, qit = col_t(0), col_t(2), col_t(3)
    tab = _bias_tab(rel_bias.astype(F32), TQ)
    att = _sparse_attn(z, qt, qit, vt, kk, wit, tab, B, S, TQ, TK, CK)

    wr = jnp.pad(w_router[0].astype(F32), ((0, 0), (0, LANES - E)))
    wr_hi = wr.astype(BF16)
    wr_lo = (wr - wr_hi.astype(F32)).astype(BF16)
    br = jnp.pad(b_router[0].astype(F32), (0, LANES - E)).reshape(1, LANES)
    tm_mix = _pick(M, (256, 128))
    h1, h1b, te, tg = _mix(x2, att, z, conv_w[0].astype(F32), w_att_br[0].astype(BF16),
                           w_conv_br[0].astype(BF16), w_o[0].astype(BF16),
                           row(ln_in_g), row(ln_in_b), row(ln1_g[0]), row(ln1_b[0]), wr_hi, wr_lo, br,
                           S, tm_mix, E)

    top_e = te[:, :TOP_K_EXPERTS]
    gates = tg[:, :TOP_K_EXPERTS]
    n_assign = M * TOP_K_EXPERTS
    tm_e = _pick(n_assign, (512, 256, 128))
    e_flat = top_e.reshape(-1)
    order = jnp.argsort(e_flat, stable=True).astype(jnp.int32)
    onehot = (e_flat[:, None] == jnp.arange(E, dtype=jnp.int32)[None, :]).astype(jnp.int32)
    csum = jnp.cumsum(onehot, axis=0)
    counts = csum[-1]
    rank = jnp.take_along_axis(csum, e_flat[:, None], axis=1)[:, 0] - 1
    starts = jnp.cumsum(counts) - counts
    padded = (counts + tm_e - 1) // tm_e * tm_e
    pends = jnp.cumsum(padded)
    pstarts = pends - padded
    pos = pstarts[e_flat] + rank
    n_blocks = (n_assign + E * (tm_e - 1) + tm_e - 1) // tm_e
    n_rows = n_blocks * tm_e
    r = jnp.arange(n_rows, dtype=jnp.int32)
    row_e = jnp.minimum(jnp.searchsorted(pends, r, side='right'), E - 1).astype(jnp.int32)
    j = r - pstarts[row_e]
    valid = j < counts[row_e]
    src = jnp.clip(starts[row_e] + j, 0, n_assign - 1)
    a_id = order[src]
    row_tok = jnp.where(valid, a_id // TOP_K_EXPERTS, 0)
    row_w = jnp.where(valid, gates.reshape(-1)[a_id], 0.0)
    n_used = pends[-1] // tm_e
    blk = jnp.arange(n_blocks, dtype=jnp.int32)
    block_used = (blk < n_used).astype(jnp.int32)
    block_e = row_e.reshape(n_blocks, tm_e)[:, 0][jnp.minimum(blk, n_used - 1)]

    xs = h1b[row_tok]
    rw = jnp.broadcast_to(row_w[:, None], (n_rows, LANES))
    F = w_gate.shape[-1]
    tf = _pick(F, (512, 256, 128))
    eo = _moe(block_e, block_used, xs, rw,
              _cast_bf16(w_gate[0]), b_gate[0].reshape(E, 1, F).astype(F32),
              _cast_bf16(w_up[0]), b_up[0].reshape(E, 1, F).astype(F32),
              _cast_bf16(w_down[0]), b_down[0].reshape(E, 1, D).astype(F32), tm_e, tf)

    ffg = eo[pos.reshape(M, TOP_K_EXPERTS).T]

    tm_p = _pick(M, (256, 128))
    out = _ple(h1, h1b, ffg, p[0].reshape(M, -1), w_ple_gate[0].astype(BF16),
               w_ple_proj[0].astype(BF16), row(ln2_g[0]), row(ln2_b[0]), tm_p)
    return out.reshape(B, S, D)
```

```python
import functools
import math

import jax
import jax.numpy as jnp
from jax import lax
from jax.experimental import pallas as pl
from jax.experimental.pallas import tpu as pltpu

CHUNK = 64
CHUNK_SHIFT = 6
assert 1 << CHUNK_SHIFT == CHUNK
ATT_HEADS = 8
ATT_HEAD_DIM = 128
ATT_WIDTH = ATT_HEADS * ATT_HEAD_DIM
IDX_HEADS = 16
IDX_DIM = 64
MAX_TOPK = 256
CONV_WIDTH = 1024
CONV_K = 3
REL_BUCKETS = 32
REL_MAX_DIST = 128
TOP_K_EXPERTS = 4
SWIGLU_LIMIT = 7.0
SWIGLU_ALPHA = 1.702
LN_EPS = 1e-5
DEPTH = 1
DEEPNORM_ALPHA = (2 * DEPTH) ** 0.25

LANES = 128
MOE_SUB = 256
VMEM_LIMIT_BYTES = 56 * 1024 * 1024

LOG2E = math.log2(math.e)
INT_MIN = -(2 ** 31)
KEY_NEG_INF = (0xFF800000 ^ 0x7FFFFFFF) - (1 << 32)

F32 = jnp.float32
BF16 = jnp.bfloat16


def _cparams(sem):
    return pltpu.CompilerParams(dimension_semantics=sem, vmem_limit_bytes=VMEM_LIMIT_BYTES)


def _ln(xf, g, b):
    mu = jnp.mean(xf, axis=-1, keepdims=True)
    xc = xf - mu
    var = jnp.mean(xc * xc, axis=-1, keepdims=True)
    return xc * lax.rsqrt(var + LN_EPS) * g + b


def _rep_lanes(v, rep):
    return v if rep == 1 else jnp.concatenate([v] * rep, axis=1)


def _resident(shape, index_map):
    return pl.BlockSpec(shape, index_map, pipeline_mode=pl.Buffered(1))


def _bias_tab_kernel(rb_ref, o_ref, *, T):
    c = pl.program_id(0)
    h = pl.program_id(1)
    half = REL_BUCKETS // 2
    max_exact = half // 2
    j = lax.broadcasted_iota(jnp.int32, (T, T), 0)
    i = lax.broadcasted_iota(jnp.int32, (T, T), 1)
    rel = j - i - c * T
    ret = jnp.where(rel > 0, half, 0)
    n = jnp.abs(rel)
    nf = jnp.maximum(n, 1).astype(F32)
    large = max_exact + (jnp.log(nf / max_exact) / math.log(REL_MAX_DIST / max_exact)
                         * (half - max_exact)).astype(jnp.int32)
    large = jnp.minimum(large, half - 1)
    bucket = ret + jnp.where(n < max_exact, n, large)
    far = rb_ref[half - 1, h]
    acc = jnp.zeros((T, T), F32)
    for b in range(REL_BUCKETS):
        acc = jnp.where(bucket == b, rb_ref[b, h] - far, acc)
    o_ref[...] = acc * LOG2E


def _bias_tab(rel_bias, T):
    return pl.pallas_call(
        functools.partial(_bias_tab_kernel, T=T),
        grid=(2, ATT_HEADS),
        in_specs=[pl.BlockSpec(memory_space=pltpu.SMEM)],
        out_specs=pl.BlockSpec((None, None, T, T), lambda c, h: (c, h, 0, 0)),
        out_shape=jax.ShapeDtypeStruct((2, ATT_HEADS, T, T), F32),
        compiler_params=_cparams(("arbitrary", "arbitrary")),
        name="bias_tab",
    )(rel_bias)


def _inproj_kernel(x_ref, g_ref, b_ref, w_ref, ws_ref, z_ref, zs_ref, hn_sc):
    j = pl.program_id(1)

    @pl.when(j == 0)
    def _():
        hn = _ln(x_ref[...], g_ref[...], b_ref[...]).astype(BF16)
        hn_sc[...] = hn
        zs_ref[...] = jnp.dot(hn, ws_ref[...], preferred_element_type=F32)

    z_ref[...] = jnp.dot(hn_sc[...], w_ref[...], preferred_element_type=F32).astype(BF16)


def _inproj(x2, g, b, w_main, w_small, tm, tn):
    M, D = x2.shape
    N = w_main.shape[1]
    return pl.pallas_call(
        _inproj_kernel,
        grid=(M // tm, N // tn),
        in_specs=[
            pl.BlockSpec((tm, D), lambda i, j: (i, 0)),
            pl.BlockSpec((1, D), lambda i, j: (0, 0)),
            pl.BlockSpec((1, D), lambda i, j: (0, 0)),
            pl.BlockSpec((D, tn), lambda i, j: (0, j)),
            pl.BlockSpec((D, LANES), lambda i, j: (0, 0)),
        ],
        out_specs=[
            pl.BlockSpec((tm, tn), lambda i, j: (i, j)),
            pl.BlockSpec((tm, LANES), lambda i, j: (i, 0)),
        ],
        out_shape=[
            jax.ShapeDtypeStruct((M, N), BF16),
            jax.ShapeDtypeStruct((M, LANES), F32),
        ],
        scratch_shapes=[pltpu.VMEM((tm, D), BF16)],
        compiler_params=_cparams(("parallel", "arbitrary")),
        name="ln_inproj",
    )(x2, g, b, w_main, w_small)


def _attn_kernel(qb_ref, kt_ref, qt_ref, qit_ref, em_ref, om_ref, wi_ref, kk_ref, k_ref, vt_ref, tab_ref,
                 o_ref, key_sc, qe_sc, qo_sc, thr_sc, m_sc, l_sc, acc_sc, madd_sc, x_sc, p_sc,
                 *, TQ, TK, CK, S, topk):
    qb = qb_ref[pl.program_id(1)]
    kt = kt_ref[pl.program_id(1)]
    t0 = qb * TQ
    lim_max = t0 + TQ
    last_kt = (lim_max - 1) // TK
    n_sub = TK // TQ
    SLAB = 32

    @pl.when(kt == 0)
    def _phase1():
        qi = qit_ref[...]
        qe_sc[...] = qi * em_ref[...]
        qo_sc[...] = qi * om_ref[...]
        n_ck = (lim_max + CK - 1) // CK
        q_idx = t0 + lax.broadcasted_iota(jnp.int32, (1, TQ), 1)
        limit = ((q_idx >> CHUNK_SHIFT) + 1) << CHUNK_SHIFT

        def score_chunk(c, carry):
            c0 = pl.multiple_of(c * CK, CK)
            kk = kk_ref[pl.ds(c0, CK), :]
            acc = jnp.zeros((CK, TQ), F32)
            for jp in range(IDX_HEADS // 2):
                sl = slice(LANES * jp, LANES * (jp + 1))
                le = jnp.dot(kk, qe_sc[sl, :], preferred_element_type=F32)
                lo = jnp.dot(kk, qo_sc[sl, :], preferred_element_type=F32)
                acc = acc + jnp.maximum(le, 0.0) * wi_ref[2 * jp:2 * jp + 1, :]
                acc = acc + jnp.maximum(lo, 0.0) * wi_ref[2 * jp + 1:2 * jp + 2, :]
            s_idx = c0 + lax.broadcasted_iota(jnp.int32, (CK, TQ), 0)
            sc = jnp.where(s_idx < limit, acc, -jnp.inf)
            bits = pltpu.bitcast(sc, jnp.int32)
            key_sc[pl.ds(c0, CK), :] = bits ^ ((bits >> 31) & 0x7FFFFFFF)
            return carry

        lax.fori_loop(0, n_ck, score_chunk, 0)

        def count(pred):
            def chunk(c, acc):
                c0 = pl.multiple_of(c * CK, CK)
                for r in range(CK // SLAB):
                    blk = key_sc[pl.ds(c0 + SLAB * r, SLAB), :]
                    acc = acc + pred(blk, c0 + SLAB * r).astype(jnp.int32)
                return acc

            acc = lax.fori_loop(0, n_ck, chunk, jnp.zeros((SLAB, TQ), jnp.int32))
            return jnp.sum(acc, axis=0, keepdims=True)

        def bit_step(state):
            i, u, c_u, settled, _ = state
            bit = jnp.left_shift(jnp.int32(1), 31 - i)
            cand_u = u | bit
            cand = jnp.broadcast_to(cand_u ^ INT_MIN, (SLAB, TQ))
            cnt = count(lambda blk, _: blk >= cand)
            take = cnt >= topk
            settled = jnp.where(cnt == topk, 1, settled)
            return (i + 1, jnp.where(take, cand_u, u), jnp.where(take, cnt, c_u), settled,
                    jnp.min(settled))

        settled0 = (limit <= topk).astype(jnp.int32)
        zero = jnp.zeros((1, TQ), jnp.int32)
        state = (jnp.int32(0), zero, zero + topk, settled0, jnp.min(settled0))
        _, u, c_u, _, _ = lax.while_loop(lambda s: (s[0] < 32) & (s[4] == 0), bit_step, state)
        t_star = u ^ INT_MIN
        thr_sc[...] = jnp.broadcast_to(jnp.maximum(t_star, KEY_NEG_INF + 1), thr_sc.shape)

        tie = (c_u > topk) & (t_star > KEY_NEG_INF)

        @pl.when(jnp.max(tie.astype(jnp.int32)) > 0)
        def _ties():
            t_b = jnp.broadcast_to(t_star, (SLAB, TQ))
            need = topk - count(lambda blk, _: blk > t_b)
            row = lax.broadcasted_iota(jnp.int32, (SLAB, TQ), 0)

            def idx_step(i, v):
                cand_v = v | jnp.left_shift(jnp.int32(1), S.bit_length() - 1 - i)
                cand_b = jnp.broadcast_to(cand_v, (SLAB, TQ))
                cnt = count(lambda blk, s0: (blk == t_b) & (row + s0 < cand_b))
                return jnp.where(cnt < need, cand_v, v)

            v = lax.fori_loop(0, S.bit_length(), idx_step, zero)
            drop_from = jnp.broadcast_to(jnp.where(tie, v + 1, S), (SLAB, TQ))

            def fix_chunk(c, carry):
                c0 = pl.multiple_of(c * CK, CK)
                for r in range(CK // SLAB):
                    rows = pl.ds(c0 + SLAB * r, SLAB)
                    blk = key_sc[rows, :]
                    drop = (blk == t_b) & (row + (c0 + SLAB * r) >= drop_from)
                    key_sc[rows, :] = jnp.where(drop, blk - 1, blk)
                return carry

            lax.fori_loop(0, n_ck, fix_chunk, 0)

        m_sc[...] = jnp.full(m_sc.shape, -1e30, F32)
        l_sc[...] = jnp.zeros(l_sc.shape, F32)
        acc_sc[...] = jnp.zeros(acc_sc.shape, F32)

    def attend(s0, r0, bias_idx):
        madd_sc[...] = jnp.where(key_sc[pl.ds(s0, TQ), :] >= thr_sc[0:1, :], 0.0, -jnp.inf)

        def logits(h):
            hs = slice(ATT_HEAD_DIM * h, ATT_HEAD_DIM * (h + 1))
            x = jnp.dot(k_ref[pl.ds(r0, TQ), hs], qt_ref[hs, :],
                        preferred_element_type=F32) + madd_sc[...]
            if bias_idx is not None:
                x = x + tab_ref[bias_idx, h]
            x_sc[h % 3] = x
            return jnp.max(x, axis=0, keepdims=True)

        def weighted_values(h, alpha):
            hs = slice(ATT_HEAD_DIM * h, ATT_HEAD_DIM * (h + 1))
            pv = jnp.dot(vt_ref[hs, pl.ds(r0, TQ)], p_sc[h % 2], preferred_element_type=F32)
            acc_sc[hs, :] = acc_sc[hs, :] * alpha + pv

        m_cur = {0: logits(0), 1: logits(1)}
        alphas = {}
        for h in range(ATT_HEADS):
            if h + 2 < ATT_HEADS:
                m_cur[h + 2] = logits(h + 2)
            if h >= 1:
                weighted_values(h - 1, alphas.pop(h - 1))
            m_old = m_sc[h:h + 1, :]
            m_new = jnp.maximum(m_old, m_cur.pop(h))
            alpha = jnp.exp2(m_old - m_new)
            p = jnp.exp2(x_sc[h % 3] - m_new)
            l_sc[h:h + 1, :] = alpha * l_sc[h:h + 1, :] + jnp.sum(p, axis=0, keepdims=True)
            p_sc[h % 2] = p.astype(BF16)
            m_sc[h:h + 1, :] = m_new
            alphas[h] = alpha
        weighted_values(ATT_HEADS - 1, alphas.pop(ATT_HEADS - 1))

    @pl.when(kt <= last_kt)
    def _phase2():
        def sub(c, carry):
            r0 = pl.multiple_of(c * TQ, TQ)
            s0 = pl.multiple_of(kt * TK + r0, TQ)
            d = s0 - t0

            @pl.when(d == 0)
            def _():
                attend(s0, r0, 0)

            @pl.when(d == -TQ)
            def _():
                attend(s0, r0, 1)

            @pl.when(d < -TQ)
            def _():
                attend(s0, r0, None)

            return carry

        lax.fori_loop(0, n_sub, sub, 0)

    @pl.when(kt == last_kt)
    def _finish():
        for h in range(ATT_HEADS):
            hs = slice(ATT_HEAD_DIM * h, ATT_HEAD_DIM * (h + 1))
            o_ref[:, hs] = (acc_sc[hs, :] / l_sc[h:h + 1, :]).T.astype(o_ref.dtype)


def _sparse_attn(z, qt, qit, vt, kk, wit, tab, B, S, TQ, TK, CK):
    nq = S // TQ
    nk = S // TK
    topk = min(MAX_TOPK, S // 4)
    feat = jnp.arange(ATT_WIDTH, dtype=jnp.int32)[:, None]
    em = jnp.broadcast_to(((feat // IDX_DIM) % 2 == 0).astype(BF16), (ATT_WIDTH, TQ))
    om = jnp.broadcast_to(((feat // IDX_DIM) % 2 == 1).astype(BF16), (ATT_WIDTH, TQ))

    pairs = [(i, k) for i in range(nq) for k in range(((i + 1) * TQ - 1) // TK + 1)]
    qb_tab = jnp.asarray([i for i, _ in pairs], jnp.int32)
    kt_tab = jnp.asarray([k for _, k in pairs], jnp.int32)

    kern = functools.partial(_attn_kernel, TQ=TQ, TK=TK, CK=CK, S=S, topk=topk)
    grid_spec = pltpu.PrefetchScalarGridSpec(
        num_scalar_prefetch=2,
        grid=(B, len(pairs)),
        in_specs=[
            pl.BlockSpec((None, ATT_WIDTH, TQ), lambda b, s, qb, kt: (b, 0, qb[s])),
            pl.BlockSpec((None, ATT_WIDTH, TQ), lambda b, s, qb, kt: (b, 0, qb[s])),
            _resident((ATT_WIDTH, TQ), lambda b, s, qb, kt: (0, 0)),
            _resident((ATT_WIDTH, TQ), lambda b, s, qb, kt: (0, 0)),
            pl.BlockSpec((None, IDX_HEADS, TQ), lambda b, s, qb, kt: (b, 0, qb[s])),
            pl.BlockSpec((S, LANES), lambda b, s, qb, kt: (b, 0), pipeline_mode=pl.Buffered(1)),
            pl.BlockSpec((TK, ATT_WIDTH), lambda b, s, qb, kt: (b * nk + kt[s], 1)),
            pl.BlockSpec((None, ATT_WIDTH, TK), lambda b, s, qb, kt: (b, 0, kt[s])),
            _resident((2, ATT_HEADS, TQ, TQ), lambda b, s, qb, kt: (0, 0, 0, 0)),
        ],
        out_specs=pl.BlockSpec((TQ, ATT_WIDTH), lambda b, s, qb, kt: (b * nq + qb[s], 0)),
        scratch_shapes=[
            pltpu.VMEM((S, TQ), jnp.int32),
            pltpu.VMEM((ATT_WIDTH, TQ), BF16),
            pltpu.VMEM((ATT_WIDTH, TQ), BF16),
            pltpu.VMEM((8, TQ), jnp.int32),
            pltpu.VMEM((ATT_HEADS, TQ), F32),
            pltpu.VMEM((ATT_HEADS, TQ), F32),
            pltpu.VMEM((ATT_WIDTH, TQ), F32),
            pltpu.VMEM((TQ, TQ), F32),
            pltpu.VMEM((3, TQ, TQ), F32),
            pltpu.VMEM((2, TQ, TQ), BF16),
        ],
    )
    return pl.pallas_call(
        kern,
        grid_spec=grid_spec,
        out_shape=jax.ShapeDtypeStruct((B * S, ATT_WIDTH), BF16),
        compiler_params=_cparams(("parallel", "arbitrary")),
        name="sparse_attn",
    )(qb_tab, kt_tab, qt, qit, em, om, wit, kk, z, vt, tab)


def _mix_kernel(x_ref, att_ref, gb_ref, gc_ref, u_ref, gch_ref, uh_ref, ga_ref, gv_ref,
                cw_ref, wa_ref, wb_ref, wo_ref, ling_ref, linb_ref, l1g_ref, l1b_ref,
                wrh_ref, wrl_ref, br_ref, h1_ref, h1b_ref, te_ref, tg_ref, *, tm, S, HALO, n_exp):
    i = pl.program_id(0)
    seq_start = (i * tm) % S == 0
    cu = gc_ref[...].astype(F32) * u_ref[...].astype(F32)
    cuh = gch_ref[...].astype(F32) * uh_ref[...].astype(F32)
    cuh = jnp.where(seq_start, 0.0, cuh)
    ext = jnp.concatenate([cuh, cu], axis=0)
    y = (cw_ref[2:3, :] * cu
         + cw_ref[1:2, :] * ext[HALO - 1:HALO - 1 + tm]
         + cw_ref[0:1, :] * ext[HALO - 2:HALO - 2 + tm])
    conv = (gb_ref[...].astype(F32) * y).astype(BF16)
    a = jnp.dot(att_ref[...], wa_ref[...], preferred_element_type=F32)
    c = jnp.dot(conv, wb_ref[...], preferred_element_type=F32)
    merged = (jax.nn.sigmoid(ga_ref[...].astype(F32)) * a
              + jax.nn.sigmoid(gv_ref[...].astype(F32)) * c).astype(BF16)
    mix = jnp.dot(merged, wo_ref[...], preferred_element_type=F32)
    h = _ln(x_ref[...], ling_ref[...], linb_ref[...])
    h1 = _ln(DEEPNORM_ALPHA * h + mix, l1g_ref[...], l1b_ref[...])
    h1_ref[...] = h1
    h1b_ref[...] = h1.astype(BF16)

    h1_hi = h1.astype(BF16)
    h1_lo = (h1 - h1_hi.astype(F32)).astype(BF16)
    logits = (jnp.dot(h1_hi, wrh_ref[...], preferred_element_type=F32)
              + (jnp.dot(h1_hi, wrl_ref[...], preferred_element_type=F32)
                 + jnp.dot(h1_lo, wrh_ref[...], preferred_element_type=F32))) + br_ref[...]
    lane = lax.broadcasted_iota(jnp.int32, (tm, LANES), 1)
    work = jnp.where(lane < n_exp, logits, -jnp.inf)
    te = jnp.zeros((tm, LANES), jnp.int32)
    tv = jnp.zeros((tm, LANES), F32)
    v0 = None
    den = jnp.zeros((tm, 1), F32)
    for k in range(TOP_K_EXPERTS):
        vk = jnp.max(work, axis=1, keepdims=True)
        ik = jnp.min(jnp.where(work == vk, lane, LANES), axis=1, keepdims=True)
        if k == 0:
            v0 = vk
        ek = jnp.exp(vk - v0)
        den = den + ek
        te = jnp.where(lane == k, ik, te)
        tv = jnp.where(lane == k, ek, tv)
        work = jnp.where(lane == ik, -jnp.inf, work)
    te_ref[...] = te
    tg_ref[...] = tv / den


def _mix(x2, att, z, conv_w, wa, wb, wo, ling, linb, l1g, l1b, wr_hi, wr_lo, br, S, tm, n_exp):
    M, D = x2.shape
    C = CONV_WIDTH
    HALO = 16
    hb = tm // HALO
    ga_blk = 4 * C // D
    gb_blk = (4 * C + 2 * D) // C
    kern = functools.partial(_mix_kernel, tm=tm, S=S, HALO=HALO, n_exp=n_exp)
    halo_map = lambda col: (lambda i: (jnp.maximum(i * hb - 1, 0), col))
    return pl.pallas_call(
        kern,
        grid=(M // tm,),
        in_specs=[
            pl.BlockSpec((tm, D), lambda i: (i, 0)),
            pl.BlockSpec((tm, ATT_WIDTH), lambda i: (i, 0)),
            pl.BlockSpec((tm, C), lambda i: (i, gb_blk)),
            pl.BlockSpec((tm, C), lambda i: (i, gb_blk + 1)),
            pl.BlockSpec((tm, C), lambda i: (i, gb_blk + 2)),
            pl.BlockSpec((HALO, C), halo_map(gb_blk + 1)),
            pl.BlockSpec((HALO, C), halo_map(gb_blk + 2)),
            pl.BlockSpec((tm, D), lambda i: (i, ga_blk)),
            pl.BlockSpec((tm, D), lambda i: (i, ga_blk + 1)),
            _resident((CONV_K, C), lambda i: (0, 0)),
            _resident((ATT_WIDTH, D), lambda i: (0, 0)),
            _resident((C, D), lambda i: (0, 0)),
            _resident((D, D), lambda i: (0, 0)),
            _resident((1, D), lambda i: (0, 0)),
            _resident((1, D), lambda i: (0, 0)),
            _resident((1, D), lambda i: (0, 0)),
            _resident((1, D), lambda i: (0, 0)),
            _resident((D, LANES), lambda i: (0, 0)),
            _resident((D, LANES), lambda i: (0, 0)),
            _resident((1, LANES), lambda i: (0, 0)),
        ],
        out_specs=[
            pl.BlockSpec((tm, D), lambda i: (i, 0)),
            pl.BlockSpec((tm, D), lambda i: (i, 0)),
            pl.BlockSpec((tm, LANES), lambda i: (i, 0)),
            pl.BlockSpec((tm, LANES), lambda i: (i, 0)),
        ],
        out_shape=[
            jax.ShapeDtypeStruct((M, D), F32),
            jax.ShapeDtypeStruct((M, D), BF16),
            jax.ShapeDtypeStruct((M, LANES), jnp.int32),
            jax.ShapeDtypeStruct((M, LANES), F32),
        ],
        compiler_params=_cparams(("parallel",)),
        name="mix_ln1",
    )(x2, att, z, z, z, z, z, z, z, conv_w, wa, wb, wo, ling, linb, l1g, l1b, wr_hi, wr_lo, br)


def _moe_kernel(be_ref, bv_ref, xs_ref, rw_ref, wg_ref, bg_ref, wu_ref, bu_ref, wd_ref, bd_ref, o_ref, acc_sc):
    i = pl.program_id(0)
    f = pl.program_id(1)
    last_f = pl.num_programs(1) - 1
    used = bv_ref[i] > 0

    @pl.when(used)
    def _():
        @pl.when(f == 0)
        def _():
            acc_sc[...] = jnp.zeros(acc_sc.shape, F32)

        xb = xs_ref[...]
        tf = wg_ref.shape[1]
        n_sub = max(1, tf // MOE_SUB)
        sub = tf // n_sub

        def gate_up(c):
            cs = slice(sub * c, sub * (c + 1))
            g = jnp.dot(xb, wg_ref[:, cs], preferred_element_type=F32) + bg_ref[:, cs]
            u = jnp.dot(xb, wu_ref[:, cs], preferred_element_type=F32) + bu_ref[:, cs]
            return g, u

        def act_down(gu, c):
            g, u = gu
            g = jnp.minimum(g, SWIGLU_LIMIT)
            u = jnp.clip(u, -SWIGLU_LIMIT, SWIGLU_LIMIT)
            a = (u + 1.0) * (g * jax.nn.sigmoid(SWIGLU_ALPHA * g))
            return jnp.dot(a.astype(BF16), wd_ref[sub * c:sub * (c + 1), :], preferred_element_type=F32)

        pending = gate_up(0)
        d = None
        for c in range(n_sub):
            nxt = gate_up(c + 1) if c + 1 < n_sub else None
            dc = act_down(pending, c)
            d = dc if d is None else d + dc
            pending = nxt
        acc_sc[...] += d

        @pl.when(f == last_f)
        def _():
            o_ref[...] = ((acc_sc[...] + bd_ref[...]) * rw_ref[:, 0:1]).astype(o_ref.dtype)

    @pl.when(jnp.logical_not(used) & (f == last_f))
    def _():
        o_ref[...] = jnp.zeros(o_ref.shape, o_ref.dtype)


def _cast_kernel(*refs):
    n = len(refs) // 2
    for x_ref, o_ref in zip(refs[:n], refs[n:]):
        o_ref[...] = x_ref[...].astype(o_ref.dtype)


def _cast_bf16(ws):
    E, A, Bc = ws[0].shape
    assert all(w.shape == (E, A, Bc) for w in ws)
    ta = _pick(A, (512, 256, 128))
    spec = pl.BlockSpec((None, ta, Bc), lambda e, a: (e, a, 0))
    return pl.pallas_call(
        _cast_kernel,
        grid=(E, A // ta),
        in_specs=[spec] * len(ws),
        out_specs=[spec] * len(ws),
        out_shape=[jax.ShapeDtypeStruct(w.shape, BF16) for w in ws],
        compiler_params=_cparams(("parallel", "parallel")),
        name="cast_bf16",
    )(*ws)


def _moe(block_e, block_used, xs, rw, wg, bg, wu, bu, wd, bd, tm, tf):
    R, D = xs.shape
    F = wg.shape[2]
    nf = F // tf

    def f_blk(i, f, bv):
        return jnp.where(bv[i] > 0, f, nf - 1)

    grid_spec = pltpu.PrefetchScalarGridSpec(
        num_scalar_prefetch=2,
        grid=(R // tm, nf),
        in_specs=[
            pl.BlockSpec((tm, D), lambda i, f, be, bv: (i, 0)),
            pl.BlockSpec((tm, LANES), lambda i, f, be, bv: (i, 0)),
            pl.BlockSpec((None, D, tf), lambda i, f, be, bv: (be[i], 0, f_blk(i, f, bv))),
            pl.BlockSpec((None, 1, tf), lambda i, f, be, bv: (be[i], 0, f_blk(i, f, bv))),
            pl.BlockSpec((None, D, tf), lambda i, f, be, bv: (be[i], 0, f_blk(i, f, bv))),
            pl.BlockSpec((None, 1, tf), lambda i, f, be, bv: (be[i], 0, f_blk(i, f, bv))),
            pl.BlockSpec((None, tf, D), lambda i, f, be, bv: (be[i], f_blk(i, f, bv), 0)),
            pl.BlockSpec((None, 1, D), lambda i, f, be, bv: (be[i], 0, 0)),
        ],
        out_specs=pl.BlockSpec((tm, D), lambda i, f, be, bv: (i, 0)),
        scratch_shapes=[pltpu.VMEM((tm, D), F32)],
    )
    return pl.pallas_call(
        _moe_kernel,
        grid_spec=grid_spec,
        out_shape=jax.ShapeDtypeStruct((R, D), BF16),
        compiler_params=_cparams(("parallel", "arbitrary")),
        name="moe_experts",
    )(block_e, block_used, xs, rw, wg, bg, wu, bu, wd, bd)


def _ple_kernel(h1_ref, h1b_ref, ff_ref, p_ref, wpg_ref, wpp_ref, g_ref, b_ref, o_ref):
    h1 = h1_ref[...]
    ff = ff_ref[0].astype(F32)
    for k in range(1, TOP_K_EXPERTS):
        ff = ff + ff_ref[k].astype(F32)
    gate = jax.nn.sigmoid(jnp.dot(h1b_ref[...], wpg_ref[...], preferred_element_type=F32))
    proj = jnp.dot(p_ref[...].astype(BF16), wpp_ref[...], preferred_element_type=F32)
    o_ref[...] = _ln(DEEPNORM_ALPHA * h1 + ff + gate * proj, g_ref[...], b_ref[...])


def _ple(h1, h1b, ffg, p2, wpg, wpp, g, b, tm):
    M, D = h1.shape
    P = p2.shape[1]
    return pl.pallas_call(
        _ple_kernel,
        grid=(M // tm,),
        in_specs=[
            pl.BlockSpec((tm, D), lambda i: (i, 0)),
            pl.BlockSpec((tm, D), lambda i: (i, 0)),
            pl.BlockSpec((TOP_K_EXPERTS, tm, D), lambda i: (0, i, 0)),
            pl.BlockSpec((tm, P), lambda i: (i, 0)),
            _resident((D, D), lambda i: (0, 0)),
            _resident((P, D), lambda i: (0, 0)),
            _resident((1, D), lambda i: (0, 0)),
            _resident((1, D), lambda i: (0, 0)),
        ],
        out_specs=pl.BlockSpec((tm, D), lambda i: (i, 0)),
        out_shape=jax.ShapeDtypeStruct((M, D), F32),
        compiler_params=_cparams(("parallel",)),
        name="ple_ln2",
    )(h1, h1b, ffg, p2, wpg, wpp, g, b)


def _pick(n, prefs):
    for t in prefs:
        if n % t == 0:
            return t
    return n


def kernel(x, p, ln_in_g, ln_in_b, rel_bias, w_in, conv_w, w_att_br, w_conv_br, w_o, ln1_g, ln1_b,
           w_router, b_router, w_gate, b_gate, w_up, b_up, w_down, b_down, w_ple_gate, w_ple_proj,
           ln2_g, ln2_b):
    B, S, D = x.shape
    M = B * S
    E = w_router.shape[-1]
    assert w_in.shape[0] == DEPTH == 1
    C = CONV_WIDTH
    x2 = x.reshape(M, D)
    row = lambda v: v.reshape(1, -1).astype(F32)

    w = w_in[0]
    sp = [ATT_WIDTH, ATT_WIDTH, ATT_WIDTH, IDX_HEADS * IDX_DIM, IDX_DIM, IDX_HEADS, C, C, C, D, D]
    offs = [0]
    for s_ in sp:
        offs.append(offs[-1] + s_)
    seg = lambda a, b_: w[:, offs[a]:offs[b_]]
    assert (2 * D) % C == 0 and ATT_WIDTH == C and IDX_HEADS * IDX_DIM == C
    w_q = seg(0, 1) * (ATT_HEAD_DIM ** -0.5 * LOG2E)
    w_main = jnp.concatenate([w_q, seg(1, 4), seg(9, 11), seg(6, 9)], axis=1)
    n_main = w_main.shape[1]
    tn = _pick(n_main, (1024, 512, 256, 128))
    w_main = w_main.astype(BF16)
    w_small = jnp.pad(seg(4, 6), ((0, 0), (0, LANES - IDX_DIM - IDX_HEADS))).astype(BF16)

    tm_in = _pick(M, (1024, 512, 256, 128))
    z, zs = _inproj(x2, row(ln_in_g), row(ln_in_b), w_main, w_small, tm_in, tn)

    TQ = _pick(S, (256, 128))
    TK = _pick(S, (1024, 512, 256))
    CK = _pick(S, (512, 256))
    ki = zs[:, :IDX_DIM].astype(BF16)
    kk = jnp.concatenate([ki, ki], axis=1)
    wit = jnp.swapaxes(zs[:, IDX_DIM:IDX_DIM + IDX_HEADS].reshape(B, S, IDX_HEADS), 1, 2)
    col_t = lambda c: jnp.swapaxes(z[:, c * ATT_WIDTH:(c + 1) * ATT_WIDTH].reshape(B, S, ATT_WIDTH), 1, 2)
    qt, vt, qit = col_t(0), col_t(2), col_t(3)
    tab = _bias_tab(rel_bias.astype(F32), TQ)
    att = _sparse_attn(z, qt, qit, vt, kk, wit, tab, B, S, TQ, TK, CK)

    wr = jnp.pad(w_router[0].astype(F32), ((0, 0), (0, LANES - E)))
    wr_hi = wr.astype(BF16)
    wr_lo = (wr - wr_hi.astype(F32)).astype(BF16)
    br = jnp.pad(b_router[0].astype(F32), (0, LANES - E)).reshape(1, LANES)
    tm_mix = _pick(M, (256, 128))
    h1, h1b, te, tg = _mix(x2, att, z, conv_w[0].astype(F32), w_att_br[0].astype(BF16),
                           w_conv_br[0].astype(BF16), w_o[0].astype(BF16),
                           row(ln_in_g), row(ln_in_b), row(ln1_g[0]), row(ln1_b[0]), wr_hi, wr_lo, br,
                           S, tm_mix, E)

    top_e = te[:, :TOP_K_EXPERTS]
    gates = tg[:, :TOP_K_EXPERTS]
    n_assign = M * TOP_K_EXPERTS
    tm_e = _pick(n_assign, (512, 256, 128))
    e_flat = top_e.reshape(-1)
    order = jnp.argsort(e_flat, stable=True).astype(jnp.int32)
    onehot = (e_flat[:, None] == jnp.arange(E, dtype=jnp.int32)[None, :]).astype(jnp.int32)
    csum = jnp.cumsum(onehot, axis=0)
    counts = csum[-1]
    rank = jnp.take_along_axis(csum, e_flat[:, None], axis=1)[:, 0] - 1
    starts = jnp.cumsum(counts) - counts
    padded = (counts + tm_e - 1) // tm_e * tm_e
    pends = jnp.cumsum(padded)
    pstarts = pends - padded
    pos = pstarts[e_flat] + rank
    n_blocks = (n_assign + E * (tm_e - 1) + tm_e - 1) // tm_e
    n_rows = n_blocks * tm_e
    r = jnp.arange(n_rows, dtype=jnp.int32)
    row_e = jnp.minimum(jnp.sum((r[:, None] >= pends[None, :]).astype(jnp.int32), axis=1), E - 1)
    j = r - pstarts[row_e]
    valid = j < counts[row_e]
    src = jnp.clip(starts[row_e] + j, 0, n_assign - 1)
    a_id = order[src]
    row_tok = jnp.where(valid, a_id // TOP_K_EXPERTS, 0)
    row_w = jnp.where(valid, gates.reshape(-1)[a_id], 0.0)
    n_used = pends[-1] // tm_e
    blk = jnp.arange(n_blocks, dtype=jnp.int32)
    block_used = (blk < n_used).astype(jnp.int32)
    block_e = row_e.reshape(n_blocks, tm_e)[:, 0][jnp.minimum(blk, n_used - 1)]

    xs = h1b[row_tok]
    rw = jnp.broadcast_to(row_w[:, None], (n_rows, LANES))
    F = w_gate.shape[-1]
    tf = _pick(F, (512, 256, 128))
    if F == D:
        wg_b, wu_b, wd_b = _cast_bf16([w_gate[0], w_up[0], w_down[0]])
    else:
        (wg_b, wu_b), (wd_b,) = _cast_bf16([w_gate[0], w_up[0]]), _cast_bf16([w_down[0]])
    eo = _moe(block_e, block_used, xs, rw,
              wg_b, b_gate[0].reshape(E, 1, F).astype(F32),
              wu_b, b_up[0].reshape(E, 1, F).astype(F32),
              wd_b, b_down[0].reshape(E, 1, D).astype(F32), tm_e, tf)

    ffg = eo[pos.reshape(M, TOP_K_EXPERTS).T]

    tm_p = _pick(M, (256, 128))
    out = _ple(h1, h1b, ffg, p[0].reshape(M, -1), w_ple_gate[0].astype(BF16),
               w_ple_proj[0].astype(BF16), row(ln2_g[0]), row(ln2_b[0]), tm_p)
    return out.reshape(B, S, D)
```

```python
import functools
import math

import jax
import jax.numpy as jnp
from jax import lax
from jax.experimental import pallas as pl
from jax.experimental.pallas import tpu as pltpu

CHUNK = 64
CHUNK_SHIFT = 6
assert 1 << CHUNK_SHIFT == CHUNK
ATT_HEADS = 8
ATT_HEAD_DIM = 128
ATT_WIDTH = ATT_HEADS * ATT_HEAD_DIM
IDX_HEADS = 16
IDX_DIM = 64
MAX_TOPK = 256
CONV_WIDTH = 1024
CONV_K = 3
REL_BUCKETS = 32
REL_MAX_DIST = 128
TOP_K_EXPERTS = 4
SWIGLU_LIMIT = 7.0
SWIGLU_ALPHA = 1.702
LN_EPS = 1e-5
DEPTH = 1
DEEPNORM_ALPHA = (2 * DEPTH) ** 0.25

LANES = 128
BF16_ROWS = 16
V_PAD = BF16_ROWS
V_ROWS = ATT_HEAD_DIM + V_PAD
UNCHECKED_PASSES = 20
WIDE_SUBTILES = 4
VMEM_LIMIT_BYTES = 56 * 1024 * 1024

LOG2E = math.log2(math.e)
INT_MIN = -(2 ** 31)
KEY_NEG_INF = (0xFF800000 ^ 0x7FFFFFFF) - (1 << 32)

F32 = jnp.float32
BF16 = jnp.bfloat16


def _cparams(sem):
    return pltpu.CompilerParams(dimension_semantics=sem, vmem_limit_bytes=VMEM_LIMIT_BYTES)


def _ln(xf, g, b):
    mu = jnp.mean(xf, axis=-1, keepdims=True)
    xc = xf - mu
    var = jnp.mean(xc * xc, axis=-1, keepdims=True)
    return xc * lax.rsqrt(var + LN_EPS) * g + b


def _rep_lanes(v, rep):
    return v if rep == 1 else jnp.concatenate([v] * rep, axis=1)


def _resident(shape, index_map):
    return pl.BlockSpec(shape, index_map, pipeline_mode=pl.Buffered(1))


def _bias_tab_kernel(rb_ref, o_ref, *, T):
    c = pl.program_id(0)
    h = pl.program_id(1)
    half = REL_BUCKETS // 2
    max_exact = half // 2
    j = lax.broadcasted_iota(jnp.int32, (T, T), 0)
    i = lax.broadcasted_iota(jnp.int32, (T, T), 1)
    rel = j - i - c * T
    ret = jnp.where(rel > 0, half, 0)
    n = jnp.abs(rel)
    nf = jnp.maximum(n, 1).astype(F32)
    large = max_exact + (jnp.log(nf / max_exact) / math.log(REL_MAX_DIST / max_exact)
                         * (half - max_exact)).astype(jnp.int32)
    large = jnp.minimum(large, half - 1)
    bucket = ret + jnp.where(n < max_exact, n, large)
    far = rb_ref[half - 1, h]
    acc = jnp.zeros((T, T), F32)
    for b in range(REL_BUCKETS):
        acc = jnp.where(bucket == b, rb_ref[b, h] - far, acc)
    o_ref[...] = acc * LOG2E


def _bias_tab(rel_bias, T):
    return pl.pallas_call(
        functools.partial(_bias_tab_kernel, T=T),
        grid=(2, ATT_HEADS),
        in_specs=[pl.BlockSpec(memory_space=pltpu.SMEM)],
        out_specs=pl.BlockSpec((None, None, T, T), lambda c, h: (c, h, 0, 0)),
        out_shape=jax.ShapeDtypeStruct((2, ATT_HEADS, T, T), F32),
        compiler_params=_cparams(("arbitrary", "arbitrary")),
        name="bias_tab",
    )(rel_bias)


def _inproj_kernel(x_ref, g_ref, b_ref, w_ref, ws_ref, z_ref, zs_ref, hn_sc):
    j = pl.program_id(1)

    @pl.when(j == 0)
    def _():
        hn = _ln(x_ref[...], g_ref[...], b_ref[...]).astype(BF16)
        hn_sc[...] = hn
        zs_ref[...] = jnp.dot(hn, ws_ref[...], preferred_element_type=F32)

    z_ref[...] = jnp.dot(hn_sc[...], w_ref[...], preferred_element_type=F32).astype(BF16)


def _inproj(x2, g, b, w_main, w_small, tm, tn):
    M, D = x2.shape
    N = w_main.shape[1]
    return pl.pallas_call(
        _inproj_kernel,
        grid=(M // tm, N // tn),
        in_specs=[
            pl.BlockSpec((tm, D), lambda i, j: (i, 0)),
            pl.BlockSpec((1, D), lambda i, j: (0, 0)),
            pl.BlockSpec((1, D), lambda i, j: (0, 0)),
            pl.BlockSpec((D, tn), lambda i, j: (0, j)),
            pl.BlockSpec((D, LANES), lambda i, j: (0, 0)),
        ],
        out_specs=[
            pl.BlockSpec((tm, tn), lambda i, j: (i, j)),
            pl.BlockSpec((tm, LANES), lambda i, j: (i, 0)),
        ],
        out_shape=[
            jax.ShapeDtypeStruct((M, N), BF16),
            jax.ShapeDtypeStruct((M, LANES), F32),
        ],
        scratch_shapes=[pltpu.VMEM((tm, D), BF16)],
        compiler_params=_cparams(("parallel", "arbitrary")),
        name="ln_inproj",
    )(x2, g, b, w_main, w_small)


def _attn_kernel(qb_ref, kt_ref, qt_ref, qit_ref, em_ref, om_ref, wi_ref, kk_ref, k_ref, vt_ref, tab_ref,
                 o_ref, key_sc, qe_sc, qo_sc, thr_sc, m_sc, acc_sc, madd_sc, x_sc, p_sc,
                 *, TQ, TK, CK, S, topk):
    qb = qb_ref[pl.program_id(1)]
    kt = kt_ref[pl.program_id(1)]
    t0 = qb * TQ
    lim_max = t0 + TQ
    last_kt = (lim_max - 1) // TK
    n_sub = TK // TQ
    SLAB = 32

    @pl.when(kt == 0)
    def _phase1():
        qi = qit_ref[...]
        qe_sc[...] = qi * em_ref[...]
        qo_sc[...] = qi * om_ref[...]
        n_ck = (lim_max + CK - 1) // CK
        q_idx = t0 + lax.broadcasted_iota(jnp.int32, (1, TQ), 1)
        limit = ((q_idx >> CHUNK_SHIFT) + 1) << CHUNK_SHIFT

        def score_chunk(c, carry):
            c0 = pl.multiple_of(c * CK, CK)
            kk = kk_ref[pl.ds(c0, CK), :]
            acc = jnp.zeros((CK, TQ), F32)
            for jp in range(IDX_HEADS // 2):
                sl = slice(LANES * jp, LANES * (jp + 1))
                le = jnp.dot(kk, qe_sc[sl, :], preferred_element_type=F32)
                lo = jnp.dot(kk, qo_sc[sl, :], preferred_element_type=F32)
                acc = acc + jnp.maximum(le, 0.0) * wi_ref[2 * jp:2 * jp + 1, :]
                acc = acc + jnp.maximum(lo, 0.0) * wi_ref[2 * jp + 1:2 * jp + 2, :]
            s_idx = c0 + lax.broadcasted_iota(jnp.int32, (CK, TQ), 0)
            sc = jnp.where(s_idx < limit, acc, -jnp.inf)
            bits = pltpu.bitcast(sc, jnp.int32)
            key_sc[pl.ds(c0, CK), :] = bits ^ ((bits >> 31) & 0x7FFFFFFF)
            return carry

        lax.fori_loop(0, n_ck, score_chunk, 0)

        def count(pred):
            def chunk(c, acc):
                c0 = pl.multiple_of(c * CK, CK)
                for r in range(CK // SLAB):
                    blk = key_sc[pl.ds(c0 + SLAB * r, SLAB), :]
                    acc = acc + pred(blk, c0 + SLAB * r).astype(jnp.int32)
                return acc

            acc = lax.fori_loop(0, n_ck, chunk, jnp.zeros((SLAB, TQ), jnp.int32))
            return jnp.sum(acc, axis=0, keepdims=True)

        def bit_pass(i, state):
            u, c_u, settled = state
            bit = jnp.left_shift(jnp.int32(1), 31 - i)
            cand_u = u | bit
            cand = jnp.broadcast_to(cand_u ^ INT_MIN, (SLAB, TQ))
            cnt = count(lambda blk, _: blk >= cand)
            take = cnt >= topk
            settled = jnp.where(cnt == topk, 1, settled)
            return jnp.where(take, cand_u, u), jnp.where(take, cnt, c_u), settled

        def checked_pass(s):
            state = bit_pass(s[0], s[1:4])
            return (s[0] + 1,) + state + (jnp.min(state[2]),)

        settled0 = (limit <= topk).astype(jnp.int32)
        zero = jnp.zeros((1, TQ), jnp.int32)
        state = lax.fori_loop(0, UNCHECKED_PASSES, bit_pass, (zero, zero + topk, settled0))
        state = (jnp.int32(UNCHECKED_PASSES),) + state + (jnp.min(state[2]),)
        _, u, c_u, _, _ = lax.while_loop(lambda s: (s[0] < 32) & (s[4] == 0), checked_pass, state)
        t_star = u ^ INT_MIN
        thr_sc[...] = jnp.broadcast_to(jnp.maximum(t_star, KEY_NEG_INF + 1), thr_sc.shape)

        tie = (c_u > topk) & (t_star > KEY_NEG_INF)

        @pl.when(jnp.max(tie.astype(jnp.int32)) > 0)
        def _ties():
            t_b = jnp.broadcast_to(t_star, (SLAB, TQ))
            need = topk - count(lambda blk, _: blk > t_b)
            row = lax.broadcasted_iota(jnp.int32, (SLAB, TQ), 0)

            def idx_step(i, v):
                cand_v = v | jnp.left_shift(jnp.int32(1), S.bit_length() - 1 - i)
                cand_b = jnp.broadcast_to(cand_v, (SLAB, TQ))
                cnt = count(lambda blk, s0: (blk == t_b) & (row + s0 < cand_b))
                return jnp.where(cnt < need, cand_v, v)

            v = lax.fori_loop(0, S.bit_length(), idx_step, zero)
            drop_from = jnp.broadcast_to(jnp.where(tie, v + 1, S), (SLAB, TQ))

            def fix_chunk(c, carry):
                c0 = pl.multiple_of(c * CK, CK)
                for r in range(CK // SLAB):
                    rows = pl.ds(c0 + SLAB * r, SLAB)
                    blk = key_sc[rows, :]
                    drop = (blk == t_b) & (row + (c0 + SLAB * r) >= drop_from)
                    key_sc[rows, :] = jnp.where(drop, blk - 1, blk)
                return carry

            lax.fori_loop(0, n_ck, fix_chunk, 0)

        m_sc[...] = jnp.full(m_sc.shape, -1e30, F32)
        acc_sc[...] = jnp.zeros(acc_sc.shape, F32)

    def attend(s0, r0, bias_idx, nk=TQ):
        madd_sc[0:nk, :] = jnp.where(key_sc[pl.ds(s0, nk), :] >= thr_sc[0:1, :], 0.0, -jnp.inf)

        def logits(h):
            hs = slice(ATT_HEAD_DIM * h, ATT_HEAD_DIM * (h + 1))
            x = jnp.dot(k_ref[pl.ds(r0, nk), hs], qt_ref[hs, :],
                        preferred_element_type=F32) + madd_sc[0:nk, :]
            if bias_idx is not None:
                x = x + tab_ref[bias_idx, h]
            x_sc[h % 3, 0:nk, :] = x
            return jnp.max(x, axis=0, keepdims=True)

        def weighted_values(h, alpha):
            hv = slice(V_ROWS * h, V_ROWS * (h + 1))
            pv = jnp.dot(vt_ref[hv, pl.ds(r0, nk)], p_sc[h % 2, 0:nk, :], preferred_element_type=F32)
            acc_sc[hv, :] = acc_sc[hv, :] * alpha + pv

        m_cur = {0: logits(0), 1: logits(1)}
        alphas = {}
        for h in range(ATT_HEADS):
            if h + 2 < ATT_HEADS:
                m_cur[h + 2] = logits(h + 2)
            if h >= 1:
                weighted_values(h - 1, alphas.pop(h - 1))
            m_old = m_sc[h:h + 1, :]
            m_new = jnp.maximum(m_old, m_cur.pop(h))
            alpha = jnp.exp2(m_old - m_new)
            p_sc[h % 2, 0:nk, :] = jnp.exp2(x_sc[h % 3, 0:nk, :] - m_new).astype(BF16)
            m_sc[h:h + 1, :] = m_new
            alphas[h] = alpha
        weighted_values(ATT_HEADS - 1, alphas.pop(ATT_HEADS - 1))

    wide = min(WIDE_SUBTILES * TQ, TK)
    all_far = (kt + 1) * TK <= t0 - TQ

    @pl.when(all_far)
    def _phase2_far():
        def sub(c, carry):
            r0 = pl.multiple_of(c * wide, wide)
            attend(pl.multiple_of(kt * TK + r0, wide), r0, None, wide)
            return carry

        lax.fori_loop(0, TK // wide, sub, 0)

    @pl.when(jnp.logical_not(all_far))
    def _phase2():
        def sub(c, carry):
            r0 = pl.multiple_of(c * TQ, TQ)
            s0 = pl.multiple_of(kt * TK + r0, TQ)
            d = s0 - t0

            @pl.when(d == 0)
            def _():
                attend(s0, r0, 0)

            @pl.when(d == -TQ)
            def _():
                attend(s0, r0, 1)

            @pl.when(d < -TQ)
            def _():
                attend(s0, r0, None)

            return carry

        lax.fori_loop(0, n_sub, sub, 0)

    @pl.when(kt == last_kt)
    def _finish():
        for h in range(ATT_HEADS):
            hs = slice(ATT_HEAD_DIM * h, ATT_HEAD_DIM * (h + 1))
            num = acc_sc[V_ROWS * h:V_ROWS * h + ATT_HEAD_DIM, :]
            den = acc_sc[V_ROWS * h + ATT_HEAD_DIM:V_ROWS * h + ATT_HEAD_DIM + 1, :]
            o_ref[:, hs] = (num / den).T.astype(o_ref.dtype)


def _sparse_attn(z, qt, qit, vt, kk, wit, tab, B, S, TQ, TK, CK):
    nq = S // TQ
    nk = S // TK
    topk = min(MAX_TOPK, S // 4)
    wide = min(WIDE_SUBTILES * TQ, TK)
    feat =jnp.arange(ATT_WIDTH, dtype=jnp.int32)[:, None]
    em = jnp.broadcast_to(((feat // IDX_DIM) % 2 == 0).astype(BF16), (ATT_WIDTH, TQ))
    om = jnp.broadcast_to(((feat // IDX_DIM) % 2 == 1).astype(BF16), (ATT_WIDTH, TQ))

    pairs = [(i, k) for i in range(nq) for k in range(((i + 1) * TQ - 1) // TK + 1)]
    qb_tab = jnp.asarray([i for i, _ in pairs], jnp.int32)
    kt_tab = jnp.asarray([k for _, k in pairs], jnp.int32)

    kern = functools.partial(_attn_kernel, TQ=TQ, TK=TK, CK=CK, S=S, topk=topk)
    grid_spec = pltpu.PrefetchScalarGridSpec(
        num_scalar_prefetch=2,
        grid=(B, len(pairs)),
        in_specs=[
            pl.BlockSpec((None, ATT_WIDTH, TQ), lambda b, s, qb, kt: (b, 0, qb[s])),
            pl.BlockSpec((None, ATT_WIDTH, TQ), lambda b, s, qb, kt: (b, 0, qb[s])),
            _resident((ATT_WIDTH, TQ), lambda b, s, qb, kt: (0, 0)),
            _resident((ATT_WIDTH, TQ), lambda b, s, qb, kt: (0, 0)),
            pl.BlockSpec((None, IDX_HEADS, TQ), lambda b, s, qb, kt: (b, 0, qb[s])),
            pl.BlockSpec((S, LANES), lambda b, s, qb, kt: (b, 0), pipeline_mode=pl.Buffered(1)),
            pl.BlockSpec((TK, ATT_WIDTH), lambda b, s, qb, kt: (b * nk + kt[s], 1)),
            pl.BlockSpec((None, ATT_HEADS * V_ROWS, TK), lambda b, s, qb, kt: (b, 0, kt[s])),
            _resident((2, ATT_HEADS, TQ, TQ), lambda b, s, qb, kt: (0, 0, 0, 0)),
        ],
        out_specs=pl.BlockSpec((TQ, ATT_WIDTH), lambda b, s, qb, kt: (b * nq + qb[s], 0)),
        scratch_shapes=[
            pltpu.VMEM((S, TQ), jnp.int32),
            pltpu.VMEM((ATT_WIDTH, TQ), BF16),
            pltpu.VMEM((ATT_WIDTH, TQ), BF16),
            pltpu.VMEM((8, TQ), jnp.int32),
            pltpu.VMEM((ATT_HEADS, TQ), F32),
            pltpu.VMEM((ATT_HEADS * V_ROWS, TQ), F32),
            pltpu.VMEM((wide, TQ), F32),
            pltpu.VMEM((3, wide, TQ), F32),
            pltpu.VMEM((2, wide, TQ), BF16),
        ],
    )
    return pl.pallas_call(
        kern,
        grid_spec=grid_spec,
        out_shape=jax.ShapeDtypeStruct((B * S, ATT_WIDTH), BF16),
        compiler_params=_cparams(("parallel", "arbitrary")),
        name="sparse_attn",
    )(qb_tab, kt_tab, qt, qit, em, om, wit, kk, z, vt, tab)


def _mix_kernel(x_ref, att_ref, gb_ref, gc_ref, u_ref, gch_ref, uh_ref, ga_ref, gv_ref,
                cw_ref, wa_ref, wb_ref, wo_ref, ling_ref, linb_ref, l1g_ref, l1b_ref,
                wrh_ref, wrl_ref, br_ref, h1_ref, h1b_ref, te_ref, tg_ref, *, tm, S, HALO, n_exp):
    i = pl.program_id(0)
    seq_start = (i * tm) % S == 0
    cu = gc_ref[...].astype(F32) * u_ref[...].astype(F32)
    cuh = gch_ref[...].astype(F32) * uh_ref[...].astype(F32)
    cuh = jnp.where(seq_start, 0.0, cuh)
    ext = jnp.concatenate([cuh, cu], axis=0)
    y = (cw_ref[2:3, :] * cu
         + cw_ref[1:2, :] * ext[HALO - 1:HALO - 1 + tm]
         + cw_ref[0:1, :] * ext[HALO - 2:HALO - 2 + tm])
    conv = (gb_ref[...].astype(F32) * y).astype(BF16)
    a = jnp.dot(att_ref[...], wa_ref[...], preferred_element_type=F32)
    c = jnp.dot(conv, wb_ref[...], preferred_element_type=F32)
    merged = (jax.nn.sigmoid(ga_ref[...].astype(F32)) * a
              + jax.nn.sigmoid(gv_ref[...].astype(F32)) * c).astype(BF16)
    mix = jnp.dot(merged, wo_ref[...], preferred_element_type=F32)
    h = _ln(x_ref[...], ling_ref[...], linb_ref[...])
    h1 = _ln(DEEPNORM_ALPHA * h + mix, l1g_ref[...], l1b_ref[...])
    h1_ref[...] = h1
    h1b_ref[...] = h1.astype(BF16)

    h1_hi = h1.astype(BF16)
    h1_lo = (h1 - h1_hi.astype(F32)).astype(BF16)
    logits = (jnp.dot(h1_hi, wrh_ref[...], preferred_element_type=F32)
              + (jnp.dot(h1_hi, wrl_ref[...], preferred_element_type=F32)
                 + jnp.dot(h1_lo, wrh_ref[...], preferred_element_type=F32))) + br_ref[...]
    lane = lax.broadcasted_iota(jnp.int32, (tm, LANES), 1)
    work = jnp.where(lane < n_exp, logits, -jnp.inf)
    te = jnp.zeros((tm, LANES), jnp.int32)
    tv = jnp.zeros((tm, LANES), F32)
    v0 = None
    den = jnp.zeros((tm, 1), F32)
    for k in range(TOP_K_EXPERTS):
        vk = jnp.max(work, axis=1, keepdims=True)
        ik = jnp.min(jnp.where(work == vk, lane, LANES), axis=1, keepdims=True)
        if k == 0:
            v0 = vk
        ek = jnp.exp(vk - v0)
        den = den + ek
        te = jnp.where(lane == k, ik, te)
        tv = jnp.where(lane == k, ek, tv)
        work = jnp.where(lane == ik, -jnp.inf, work)
    te_ref[...] = te
    tg_ref[...] = tv / den


def _mix(x2, att, z, conv_w, wa, wb, wo, ling, linb, l1g, l1b, wr_hi, wr_lo, br, S, tm, n_exp):
    M, D = x2.shape
    C = CONV_WIDTH
    HALO = 16
    hb = tm // HALO
    ga_blk = 4 * C // D
    gb_blk = (4 * C + 2 * D) // C
    kern = functools.partial(_mix_kernel, tm=tm, S=S, HALO=HALO, n_exp=n_exp)
    halo_map = lambda col: (lambda i: (jnp.maximum(i * hb - 1, 0), col))
    return pl.pallas_call(
        kern,
        grid=(M // tm,),
        in_specs=[
            pl.BlockSpec((tm, D), lambda i: (i, 0)),
            pl.BlockSpec((tm, ATT_WIDTH), lambda i: (i, 0)),
            pl.BlockSpec((tm, C), lambda i: (i, gb_blk)),
            pl.BlockSpec((tm, C), lambda i: (i, gb_blk + 1)),
            pl.BlockSpec((tm, C), lambda i: (i, gb_blk + 2)),
            pl.BlockSpec((HALO, C), halo_map(gb_blk + 1)),
            pl.BlockSpec((HALO, C), halo_map(gb_blk + 2)),
            pl.BlockSpec((tm, D), lambda i: (i, ga_blk)),
            pl.BlockSpec((tm, D), lambda i: (i, ga_blk + 1)),
            _resident((CONV_K, C), lambda i: (0, 0)),
            _resident((ATT_WIDTH, D), lambda i: (0, 0)),
            _resident((C, D), lambda i: (0, 0)),
            _resident((D, D), lambda i: (0, 0)),
            _resident((1, D), lambda i: (0, 0)),
            _resident((1, D), lambda i: (0, 0)),
            _resident((1, D), lambda i: (0, 0)),
            _resident((1, D), lambda i: (0, 0)),
            _resident((D, LANES), lambda i: (0, 0)),
            _resident((D, LANES), lambda i: (0, 0)),
            _resident((1, LANES), lambda i: (0, 0)),
        ],
        out_specs=[
            pl.BlockSpec((tm, D), lambda i: (i, 0)),
            pl.BlockSpec((tm, D), lambda i: (i, 0)),
            pl.BlockSpec((tm, LANES), lambda i: (i, 0)),
            pl.BlockSpec((tm, LANES), lambda i: (i, 0)),
        ],
        out_shape=[
            jax.ShapeDtypeStruct((M, D), F32),
            jax.ShapeDtypeStruct((M, D), BF16),
            jax.ShapeDtypeStruct((M, LANES), jnp.int32),
            jax.ShapeDtypeStruct((M, LANES), F32),
        ],
        compiler_params=_cparams(("parallel",)),
        name="mix_ln1",
    )(x2, att, z, z, z, z, z, z, z, conv_w, wa, wb, wo, ling, linb, l1g, l1b, wr_hi, wr_lo, br)


def _moe_kernel(be_ref, bv_ref, bf_ref, xs_ref, rw_ref, wg_ref, bg_ref, wu_ref, bu_ref, wd_ref, bd_ref,
                o_ref, acc_sc, wg_sc, wu_sc, wd_sc):
    i = pl.program_id(0)
    f = pl.program_id(1)
    last_f = pl.num_programs(1) - 1
    tf = wg_ref.shape[1]
    used = bv_ref[i] > 0

    @pl.when(bf_ref[i] > 0)
    def _():
        wg_sc[f] = wg_ref[...].astype(BF16)
        wu_sc[f] = wu_ref[...].astype(BF16)
        wd_sc[f] = wd_ref[...].astype(BF16)

    @pl.when(used)
    def _():
        @pl.when(f == 0)
        def _():
            acc_sc[...] = jnp.zeros(acc_sc.shape, F32)

        xb = xs_ref[...]
        cols = pl.ds(pl.multiple_of(f * tf, tf), tf)
        g = jnp.dot(xb, wg_sc[f], preferred_element_type=F32) + bg_ref[:, cols]
        u = jnp.dot(xb, wu_sc[f], preferred_element_type=F32) + bu_ref[:, cols]
        g = jnp.minimum(g, SWIGLU_LIMIT)
        u = jnp.clip(u, -SWIGLU_LIMIT, SWIGLU_LIMIT)
        a = (u + 1.0) * (g * jax.nn.sigmoid(SWIGLU_ALPHA * g))
        acc_sc[...] += jnp.dot(a.astype(BF16), wd_sc[f], preferred_element_type=F32)

        @pl.when(f == last_f)
        def _():
            o_ref[...] = ((acc_sc[...] + bd_ref[...]) * rw_ref[:, 0:1]).astype(o_ref.dtype)

    @pl.when(jnp.logical_not(used) & (f == last_f))
    def _():
        o_ref[...] = jnp.zeros(o_ref.shape, o_ref.dtype)


def _moe(block_e, block_used, block_first, xs, rw, wg, bg, wu, bu, wd, bd, tm, tf):
    R, D = xs.shape
    F = wg.shape[2]
    nf = F // tf

    def f_blk(i, f, bf):
        return jnp.where(bf[i] > 0, f, nf - 1)

    grid_spec = pltpu.PrefetchScalarGridSpec(
        num_scalar_prefetch=3,
        grid=(R // tm, nf),
        in_specs=[
            pl.BlockSpec((tm, D), lambda i, f, be, bv, bf: (i, 0)),
            pl.BlockSpec((tm, LANES), lambda i, f, be, bv, bf: (i, 0)),
            pl.BlockSpec((None, D, tf), lambda i, f, be, bv, bf: (be[i], 0, f_blk(i, f, bf))),
            pl.BlockSpec((None, 1, F), lambda i, f, be, bv, bf: (be[i], 0, 0)),
            pl.BlockSpec((None, D, tf), lambda i, f, be, bv, bf: (be[i], 0, f_blk(i, f, bf))),
            pl.BlockSpec((None, 1, F), lambda i, f, be, bv, bf: (be[i], 0, 0)),
            pl.BlockSpec((None, tf, D), lambda i, f, be, bv, bf: (be[i], f_blk(i, f, bf), 0)),
            pl.BlockSpec((None, 1, D), lambda i, f, be, bv, bf: (be[i], 0, 0)),
        ],
        out_specs=pl.BlockSpec((tm, D), lambda i, f, be, bv, bf: (i, 0)),
        scratch_shapes=[
            pltpu.VMEM((tm, D), F32),
            pltpu.VMEM((nf, D, tf), BF16),
            pltpu.VMEM((nf, D, tf), BF16),
            pltpu.VMEM((nf, tf, D), BF16),
        ],
    )
    return pl.pallas_call(
        _moe_kernel,
        grid_spec=grid_spec,
        out_shape=jax.ShapeDtypeStruct((R, D), BF16),
        compiler_params=_cparams(("arbitrary", "arbitrary")),
        name="moe_experts",
    )(block_e, block_used, block_first, xs, rw, wg, bg, wu, bu, wd, bd)


def _ple_kernel(h1_ref, h1b_ref, ff_ref, p_ref, wpg_ref, wpp_ref, g_ref, b_ref, o_ref):
    h1 = h1_ref[...]
    ff = ff_ref[0].astype(F32)
    for k in range(1, TOP_K_EXPERTS):
        ff = ff + ff_ref[k].astype(F32)
    gate = jax.nn.sigmoid(jnp.dot(h1b_ref[...], wpg_ref[...], preferred_element_type=F32))
    proj = jnp.dot(p_ref[...].astype(BF16), wpp_ref[...], preferred_element_type=F32)
    o_ref[...] = _ln(DEEPNORM_ALPHA * h1 + ff + gate * proj, g_ref[...], b_ref[...])


def _ple(h1, h1b, ffg, p2, wpg, wpp, g, b, tm):
    M, D = h1.shape
    P = p2.shape[1]
    return pl.pallas_call(
        _ple_kernel,
        grid=(M // tm,),
        in_specs=[
            pl.BlockSpec((tm, D), lambda i: (i, 0)),
            pl.BlockSpec((tm, D), lambda i: (i, 0)),
            pl.BlockSpec((TOP_K_EXPERTS, tm, D), lambda i: (0, i, 0)),
            pl.BlockSpec((tm, P), lambda i: (i, 0)),
            _resident((D, D), lambda i: (0, 0)),
            _resident((P, D), lambda i: (0, 0)),
            _resident((1, D), lambda i: (0, 0)),
            _resident((1, D), lambda i: (0, 0)),
        ],
        out_specs=pl.BlockSpec((tm, D), lambda i: (i, 0)),
        out_shape=jax.ShapeDtypeStruct((M, D), F32),
        compiler_params=_cparams(("parallel",)),
        name="ple_ln2",
    )(h1, h1b, ffg, p2, wpg, wpp, g, b)


def _pick(n, prefs):
    for t in prefs:
        if n % t == 0:
            return t
    return n


def kernel(x, p, ln_in_g, ln_in_b, rel_bias, w_in, conv_w, w_att_br, w_conv_br, w_o, ln1_g, ln1_b,
           w_router, b_router, w_gate, b_gate, w_up, b_up, w_down, b_down, w_ple_gate, w_ple_proj,
           ln2_g, ln2_b):
    B, S, D = x.shape
    M = B * S
    E = w_router.shape[-1]
    assert w_in.shape[0] == DEPTH == 1
    C = CONV_WIDTH
    x2 = x.reshape(M, D)
    row = lambda v: v.reshape(1, -1).astype(F32)

    w = w_in[0]
    sp = [ATT_WIDTH, ATT_WIDTH, ATT_WIDTH, IDX_HEADS * IDX_DIM, IDX_DIM, IDX_HEADS, C, C, C, D, D]
    offs = [0]
    for s_ in sp:
        offs.append(offs[-1] + s_)
    seg = lambda a, b_: w[:, offs[a]:offs[b_]]
    assert (2 * D) % C == 0 and ATT_WIDTH == C and IDX_HEADS * IDX_DIM == C
    w_q = seg(0, 1) * (ATT_HEAD_DIM ** -0.5 * LOG2E)
    w_main = jnp.concatenate([w_q, seg(1, 4), seg(9, 11), seg(6, 9)], axis=1)
    n_main = w_main.shape[1]
    tn = _pick(n_main, (1024, 512, 256, 128))
    w_main = w_main.astype(BF16)
    w_small = jnp.pad(seg(4, 6), ((0, 0), (0, LANES - IDX_DIM - IDX_HEADS))).astype(BF16)

    tm_in = _pick(M, (1024, 512, 256, 128))
    z, zs = _inproj(x2, row(ln_in_g), row(ln_in_b), w_main, w_small, tm_in, tn)

    TQ = _pick(S, (256, 128))
    TK = _pick(S, (1024, 512, 256))
    CK = _pick(S, (512, 256))
    ki = zs[:, :IDX_DIM].astype(BF16)
    kk = jnp.concatenate([ki, ki], axis=1)
    wit = jnp.swapaxes(zs[:, IDX_DIM:IDX_DIM + IDX_HEADS].reshape(B, S, IDX_HEADS), 1, 2)
    col_t = lambda c: jnp.swapaxes(z[:, c * ATT_WIDTH:(c + 1) * ATT_WIDTH].reshape(B, S, ATT_WIDTH), 1, 2)
    qt, vt, qit = col_t(0), col_t(2), col_t(3)
    vt = jnp.concatenate([vt.reshape(B, ATT_HEADS, ATT_HEAD_DIM, S),
                          jnp.ones((B, ATT_HEADS, V_PAD, S), BF16)], axis=2).reshape(B, ATT_HEADS * V_ROWS, S)
    tab = _bias_tab(rel_bias.astype(F32), TQ)
    att = _sparse_attn(z, qt, qit, vt, kk, wit, tab, B, S, TQ, TK, CK)

    wr = jnp.pad(w_router[0].astype(F32), ((0, 0), (0, LANES - E)))
    wr_hi = wr.astype(BF16)
    wr_lo = (wr - wr_hi.astype(F32)).astype(BF16)
    br = jnp.pad(b_router[0].astype(F32), (0, LANES - E)).reshape(1, LANES)
    tm_mix = _pick(M, (256, 128))
    h1, h1b, te, tg = _mix(x2, att, z, conv_w[0].astype(F32), w_att_br[0].astype(BF16),
                           w_conv_br[0].astype(BF16), w_o[0].astype(BF16),
                           row(ln_in_g), row(ln_in_b), row(ln1_g[0]), row(ln1_b[0]), wr_hi, wr_lo, br,
                           S, tm_mix, E)

    top_e = te[:, :TOP_K_EXPERTS]
    gates = tg[:, :TOP_K_EXPERTS]
    n_assign = M * TOP_K_EXPERTS
    tm_e = _pick(n_assign, (512, 256, 128))
    e_flat = top_e.reshape(-1)
    order = jnp.argsort(e_flat, stable=True).astype(jnp.int32)
    onehot = (e_flat[:, None] == jnp.arange(E, dtype=jnp.int32)[None, :]).astype(jnp.int32)
    csum = jnp.cumsum(onehot, axis=0)
    counts = csum[-1]
    rank = jnp.take_along_axis(csum, e_flat[:, None], axis=1)[:, 0] - 1
    starts = jnp.cumsum(counts) - counts
    padded = (counts + tm_e - 1) // tm_e * tm_e
    pends = jnp.cumsum(padded)
    pstarts = pends - padded
    pos = pstarts[e_flat] + rank
    n_blocks = (n_assign + E * (tm_e - 1) + tm_e - 1) // tm_e
    n_rows = n_blocks * tm_e
    r = jnp.arange(n_rows, dtype=jnp.int32)
    row_e = jnp.minimum(jnp.sum((r[:, None] >= pends[None, :]).astype(jnp.int32), axis=1), E - 1)
    j = r - pstarts[row_e]
    valid = j < counts[row_e]
    src = jnp.clip(starts[row_e] + j, 0, n_assign - 1)
    a_id = order[src]
    row_tok = jnp.where(valid, a_id // TOP_K_EXPERTS, 0)
    row_w = jnp.where(valid, gates.reshape(-1)[a_id], 0.0)
    n_used = pends[-1] // tm_e
    blk = jnp.arange(n_blocks, dtype=jnp.int32)
    block_used = (blk < n_used).astype(jnp.int32)
    block_e = row_e.reshape(n_blocks, tm_e)[:, 0][jnp.minimum(blk, n_used - 1)]

    xs = h1b[row_tok]
    rw = jnp.broadcast_to(row_w[:, None], (n_rows, LANES))
    F = w_gate.shape[-1]
    tf = _pick(F, (256, 128))
    block_first = jnp.concatenate([jnp.ones((1,), jnp.int32),
                                   (block_e[1:] != block_e[:-1]).astype(jnp.int32)]) * block_used
    eo = _moe(block_e, block_used, block_first, xs, rw,
              w_gate[0].astype(F32), b_gate[0].reshape(E, 1, F).astype(F32),
              w_up[0].astype(F32), b_up[0].reshape(E, 1, F).astype(F32),
              w_down[0].astype(F32), b_down[0].reshape(E, 1, D).astype(F32), tm_e, tf)

    ffg = eo[pos.reshape(M, TOP_K_EXPERTS).T]

    tm_p = _pick(M, (256, 128))
    out = _ple(h1, h1b, ffg, p[0].reshape(M, -1), w_ple_gate[0].astype(BF16),
               w_ple_proj[0].astype(BF16), row(ln2_g[0]), row(ln2_b[0]), tm_p)
    return out.reshape(B, S, D)
```

```python
import functools
import math

import jax
import jax.numpy as jnp
from jax import lax
from jax.experimental import pallas as pl
from jax.experimental.pallas import tpu as pltpu

CHUNK = 64
CHUNK_SHIFT = 6
assert 1 << CHUNK_SHIFT == CHUNK
ATT_HEADS = 8
ATT_HEAD_DIM = 128
ATT_WIDTH = ATT_HEADS * ATT_HEAD_DIM
IDX_HEADS = 16
IDX_DIM = 64
MAX_TOPK = 256
CONV_WIDTH = 1024
CONV_K = 3
REL_BUCKETS = 32
REL_MAX_DIST = 128
TOP_K_EXPERTS = 4
SWIGLU_LIMIT = 7.0
SWIGLU_ALPHA = 1.702
LN_EPS = 1e-5
DEPTH = 1
DEEPNORM_ALPHA = (2 * DEPTH) ** 0.25

LANES = 128
BF16_ROWS = 16
V_PAD = BF16_ROWS
V_ROWS = ATT_HEAD_DIM + V_PAD
UNCHECKED_PASSES = 20
WIDE_SUBTILES = 4
VMEM_LIMIT_BYTES = 56 * 1024 * 1024

LOG2E = math.log2(math.e)
INT_MIN = -(2 ** 31)
KEY_NEG_INF = (0xFF800000 ^ 0x7FFFFFFF) - (1 << 32)

F32 = jnp.float32
BF16 = jnp.bfloat16


def _cparams(sem):
    return pltpu.CompilerParams(dimension_semantics=sem, vmem_limit_bytes=VMEM_LIMIT_BYTES)


def _ln(xf, g, b):
    mu = jnp.mean(xf, axis=-1, keepdims=True)
    xc = xf - mu
    var = jnp.mean(xc * xc, axis=-1, keepdims=True)
    return xc * lax.rsqrt(var + LN_EPS) * g + b


def _rep_lanes(v, rep):
    return v if rep == 1 else jnp.concatenate([v] * rep, axis=1)


def _resident(shape, index_map):
    return pl.BlockSpec(shape, index_map, pipeline_mode=pl.Buffered(1))


def _bias_tab_kernel(rb_ref, o_ref, *, T):
    c = pl.program_id(0)
    h = pl.program_id(1)
    half = REL_BUCKETS // 2
    max_exact = half // 2
    j = lax.broadcasted_iota(jnp.int32, (T, T), 0)
    i = lax.broadcasted_iota(jnp.int32, (T, T), 1)
    rel = j - i - c * T
    ret = jnp.where(rel > 0, half, 0)
    n = jnp.abs(rel)
    nf = jnp.maximum(n, 1).astype(F32)
    large = max_exact + (jnp.log(nf / max_exact) / math.log(REL_MAX_DIST / max_exact)
                         * (half - max_exact)).astype(jnp.int32)
    large = jnp.minimum(large, half - 1)
    bucket = ret + jnp.where(n < max_exact, n, large)
    far = rb_ref[half - 1, h]
    acc = jnp.zeros((T, T), F32)
    for b in range(REL_BUCKETS):
        acc = jnp.where(bucket == b, rb_ref[b, h] - far, acc)
    o_ref[...] = acc * LOG2E


def _bias_tab(rel_bias, T):
    return pl.pallas_call(
        functools.partial(_bias_tab_kernel, T=T),
        grid=(2, ATT_HEADS),
        in_specs=[pl.BlockSpec(memory_space=pltpu.SMEM)],
        out_specs=pl.BlockSpec((None, None, T, T), lambda c, h: (c, h, 0, 0)),
        out_shape=jax.ShapeDtypeStruct((2, ATT_HEADS, T, T), F32),
        compiler_params=_cparams(("arbitrary", "arbitrary")),
        name="bias_tab",
    )(rel_bias)


def _inproj_kernel(x_ref, g_ref, b_ref, w_ref, ws_ref, z_ref, zs_ref, hn_sc):
    j = pl.program_id(1)

    @pl.when(j == 0)
    def _():
        hn = _ln(x_ref[...], g_ref[...], b_ref[...]).astype(BF16)
        hn_sc[...] = hn
        zs_ref[...] = jnp.dot(hn, ws_ref[...], preferred_element_type=F32)

    z_ref[...] = jnp.dot(hn_sc[...], w_ref[...], preferred_element_type=F32).astype(BF16)


def _inproj(x2, g, b, w_main, w_small, tm, tn):
    M, D = x2.shape
    N = w_main.shape[1]
    return pl.pallas_call(
        _inproj_kernel,
        grid=(M // tm, N // tn),
        in_specs=[
            pl.BlockSpec((tm, D), lambda i, j: (i, 0)),
            pl.BlockSpec((1, D), lambda i, j: (0, 0)),
            pl.BlockSpec((1, D), lambda i, j: (0, 0)),
            pl.BlockSpec((D, tn), lambda i, j: (0, j)),
            pl.BlockSpec((D, LANES), lambda i, j: (0, 0)),
        ],
        out_specs=[
            pl.BlockSpec((tm, tn), lambda i, j: (i, j)),
            pl.BlockSpec((tm, LANES), lambda i, j: (i, 0)),
        ],
        out_shape=[
            jax.ShapeDtypeStruct((M, N), BF16),
            jax.ShapeDtypeStruct((M, LANES), F32),
        ],
        scratch_shapes=[pltpu.VMEM((tm, D), BF16)],
        compiler_params=_cparams(("parallel", "arbitrary")),
        name="ln_inproj",
    )(x2, g, b, w_main, w_small)


def _attn_kernel(qb_ref, kt_ref, qt_ref, qit_ref, em_ref, om_ref, wi_ref, kk_ref, k_ref, vt_ref, tab_ref,
                 o_ref, key_sc, key16_sc, qe_sc, qo_sc, thr_sc, m_sc, acc_sc, madd_sc, x_sc, p_sc,
                 *, TQ, TK, CK, S, topk):
    qb = qb_ref[pl.program_id(1)]
    kt = kt_ref[pl.program_id(1)]
    t0 = qb * TQ
    lim_max = t0 + TQ
    last_kt = (lim_max - 1) // TK
    n_sub = TK // TQ
    SLAB = 32
    SLAB16 = 2 * SLAB

    @pl.when(kt == 0)
    def _phase1():
        qi = qit_ref[...]
        qe_sc[...] = qi * em_ref[...]
        qo_sc[...] = qi * om_ref[...]
        n_ck = (lim_max + CK - 1) // CK
        q_idx = t0 + lax.broadcasted_iota(jnp.int32, (1, TQ), 1)
        limit = ((q_idx >> CHUNK_SHIFT) + 1) << CHUNK_SHIFT

        def score_chunk(c, carry):
            c0 = pl.multiple_of(c * CK, CK)
            kk = kk_ref[pl.ds(c0, CK), :]
            acc = jnp.zeros((CK, TQ), F32)
            for jp in range(IDX_HEADS // 2):
                sl = slice(LANES * jp, LANES * (jp + 1))
                le = jnp.dot(kk, qe_sc[sl, :], preferred_element_type=F32)
                lo = jnp.dot(kk, qo_sc[sl, :], preferred_element_type=F32)
                acc = acc + jnp.maximum(le, 0.0) * wi_ref[2 * jp:2 * jp + 1, :]
                acc = acc + jnp.maximum(lo, 0.0) * wi_ref[2 * jp + 1:2 * jp + 2, :]
            s_idx = c0 + lax.broadcasted_iota(jnp.int32, (CK, TQ), 0)
            sc = jnp.where(s_idx < limit, acc, -jnp.inf)
            bits = pltpu.bitcast(sc, jnp.int32)
            key = bits ^ ((bits >> 31) & 0x7FFFFFFF)
            key_sc[pl.ds(c0, CK), :] = key
            key16_sc[pl.ds(c0, CK), :] = (key >> 16).astype(jnp.int16)
            return carry

        lax.fori_loop(0, n_ck, score_chunk, 0)

        def count(pred):
            def chunk(c, acc):
                c0 = pl.multiple_of(c * CK, CK)
                for r in range(CK // SLAB):
                    blk = key_sc[pl.ds(c0 + SLAB * r, SLAB), :]
                    acc = acc + pred(blk, c0 + SLAB * r).astype(jnp.int32)
                return acc

            acc = lax.fori_loop(0, n_ck, chunk, jnp.zeros((SLAB, TQ), jnp.int32))
            return jnp.sum(acc, axis=0, keepdims=True)

        def bit_pass(i, state):
            u, c_u, settled = state
            bit = jnp.left_shift(jnp.int32(1), 31 - i)
            cand_u = u | bit
            cand = jnp.broadcast_to(cand_u ^ INT_MIN, (SLAB, TQ))
            cnt = count(lambda blk, _: blk >= cand)
            take = cnt >= topk
            settled = jnp.where(cnt == topk, 1, settled)
            return jnp.where(take, cand_u, u), jnp.where(take, cnt, c_u), settled

        def checked_pass(s):
            state = bit_pass(s[0], s[1:4])
            return (s[0] + 1,) + state + (jnp.min(state[2]),)

        settled0 = (limit <= topk).astype(jnp.int32)
        zero = jnp.zeros((1, TQ), jnp.int32)
        def high_pass(i, state):
            u, c_u, settled = state
            cand_u = u | jnp.left_shift(jnp.int32(1), 31 - i)
            cand16 = jnp.broadcast_to(((cand_u ^ INT_MIN) >> 16).astype(jnp.int16), (SLAB16, TQ))

            def chunk(c, acc):
                c0 = pl.multiple_of(c * CK, CK)
                for r in range(CK // SLAB16):
                    blk = key16_sc[pl.ds(c0 + SLAB16 * r, SLAB16), :]
                    acc = acc + (blk >= cand16).astype(jnp.int16)
                return acc

            acc = lax.fori_loop(0, n_ck, chunk, jnp.zeros((SLAB16, TQ), jnp.int16))
            cnt = jnp.sum(acc.astype(jnp.int32), axis=0, keepdims=True)
            take = cnt >= topk
            settled = jnp.where(cnt == topk, 1, settled)
            return jnp.where(take, cand_u, u), jnp.where(take, cnt, c_u), settled

        state = lax.fori_loop(0, 16, high_pass, (zero, zero + topk, settled0))
        state = lax.fori_loop(16, UNCHECKED_PASSES, bit_pass, state)
        state = (jnp.int32(UNCHECKED_PASSES),) + state + (jnp.min(state[2]),)
        _, u, c_u, _, _ = lax.while_loop(lambda s: (s[0] < 32) & (s[4] == 0), checked_pass, state)
        t_star = u ^ INT_MIN
        thr_sc[...] = jnp.broadcast_to(jnp.maximum(t_star, KEY_NEG_INF + 1), thr_sc.shape)

        tie = (c_u > topk) & (t_star > KEY_NEG_INF)

        @pl.when(jnp.max(tie.astype(jnp.int32)) > 0)
        def _ties():
            t_b = jnp.broadcast_to(t_star, (SLAB, TQ))
            need = topk - count(lambda blk, _: blk > t_b)
            row = lax.broadcasted_iota(jnp.int32, (SLAB, TQ), 0)

            def idx_step(i, v):
                cand_v = v | jnp.left_shift(jnp.int32(1), S.bit_length() - 1 - i)
                cand_b = jnp.broadcast_to(cand_v, (SLAB, TQ))
                cnt = count(lambda blk, s0: (blk == t_b) & (row + s0 < cand_b))
                return jnp.where(cnt < need, cand_v, v)

            v = lax.fori_loop(0, S.bit_length(), idx_step, zero)
            drop_from = jnp.broadcast_to(jnp.where(tie, v + 1, S), (SLAB, TQ))

            def fix_chunk(c, carry):
                c0 = pl.multiple_of(c * CK, CK)
                for r in range(CK // SLAB):
                    rows = pl.ds(c0 + SLAB * r, SLAB)
                    blk = key_sc[rows, :]
                    drop = (blk == t_b) & (row + (c0 + SLAB * r) >= drop_from)
                    key_sc[rows, :] = jnp.where(drop, blk - 1, blk)
                return carry

            lax.fori_loop(0, n_ck, fix_chunk, 0)

        m_sc[...] = jnp.full(m_sc.shape, -1e30, F32)
        acc_sc[...] = jnp.zeros(acc_sc.shape, F32)

    def attend(s0, r0, bias_idx, nk=TQ):
        madd_sc[0:nk, :] = jnp.where(key_sc[pl.ds(s0, nk), :] >= thr_sc[0:1, :], 0.0, -jnp.inf)

        def logits(h):
            hs = slice(ATT_HEAD_DIM * h, ATT_HEAD_DIM * (h + 1))
            x = jnp.dot(k_ref[pl.ds(r0, nk), hs], qt_ref[hs, :],
                        preferred_element_type=F32) + madd_sc[0:nk, :]
            if bias_idx is not None:
                x = x + tab_ref[bias_idx, h]
            x_sc[h % 3, 0:nk, :] = x
            return jnp.max(x, axis=0, keepdims=True)

        def weighted_values(h, alpha):
            hv = slice(V_ROWS * h, V_ROWS * (h + 1))
            pv = jnp.dot(vt_ref[hv, pl.ds(r0, nk)], p_sc[h % 2, 0:nk, :], preferred_element_type=F32)
            acc_sc[hv, :] = acc_sc[hv, :] * alpha + pv

        m_cur = {0: logits(0), 1: logits(1)}
        alphas = {}
        for h in range(ATT_HEADS):
            if h + 2 < ATT_HEADS:
                m_cur[h + 2] = logits(h + 2)
            if h >= 1:
                weighted_values(h - 1, alphas.pop(h - 1))
            m_old = m_sc[h:h + 1, :]
            m_new = jnp.maximum(m_old, m_cur.pop(h))
            alpha = jnp.exp2(m_old - m_new)
            p_sc[h % 2, 0:nk, :] = jnp.exp2(x_sc[h % 3, 0:nk, :] - m_new).astype(BF16)
            m_sc[h:h + 1, :] = m_new
            alphas[h] = alpha
        weighted_values(ATT_HEADS - 1, alphas.pop(ATT_HEADS - 1))

    wide = min(WIDE_SUBTILES * TQ, TK)
    all_far = (kt + 1) * TK <= t0 - TQ

    @pl.when(all_far)
    def _phase2_far():
        def sub(c, carry):
            r0 = pl.multiple_of(c * wide, wide)
            attend(pl.multiple_of(kt * TK + r0, wide), r0, None, wide)
            return carry

        lax.fori_loop(0, TK // wide, sub, 0)

    @pl.when(jnp.logical_not(all_far))
    def _phase2():
        def sub(c, carry):
            r0 = pl.multiple_of(c * TQ, TQ)
            s0 = pl.multiple_of(kt * TK + r0, TQ)
            d = s0 - t0

            @pl.when(d == 0)
            def _():
                attend(s0, r0, 0)

            @pl.when(d == -TQ)
            def _():
                attend(s0, r0, 1)

            @pl.when(d < -TQ)
            def _():
                attend(s0, r0, None)

            return carry

        lax.fori_loop(0, n_sub, sub, 0)

    @pl.when(kt == last_kt)
    def _finish():
        for h in range(ATT_HEADS):
            hs = slice(ATT_HEAD_DIM * h, ATT_HEAD_DIM * (h + 1))
            num = acc_sc[V_ROWS * h:V_ROWS * h + ATT_HEAD_DIM, :]
            den = acc_sc[V_ROWS * h + ATT_HEAD_DIM:V_ROWS * h + ATT_HEAD_DIM + 1, :]
            o_ref[:, hs] = (num / den).T.astype(o_ref.dtype)


def _sparse_attn(z, qt, qit, vt, kk, wit, tab, B, S, TQ, TK, CK):
    nq = S // TQ
    nk = S // TK
    topk = min(MAX_TOPK, S // 4)
    wide = min(WIDE_SUBTILES * TQ, TK)
    feat =jnp.arange(ATT_WIDTH, dtype=jnp.int32)[:, None]
    em = jnp.broadcast_to(((feat // IDX_DIM) % 2 == 0).astype(BF16), (ATT_WIDTH, TQ))
    om = jnp.broadcast_to(((feat // IDX_DIM) % 2 == 1).astype(BF16), (ATT_WIDTH, TQ))

    pairs = [(i, k) for i in range(nq) for k in range(((i + 1) * TQ - 1) // TK + 1)]
    qb_tab = jnp.asarray([i for i, _ in pairs], jnp.int32)
    kt_tab = jnp.asarray([k for _, k in pairs], jnp.int32)

    kern = functools.partial(_attn_kernel, TQ=TQ, TK=TK, CK=CK, S=S, topk=topk)
    grid_spec = pltpu.PrefetchScalarGridSpec(
        num_scalar_prefetch=2,
        grid=(B, len(pairs)),
        in_specs=[
            pl.BlockSpec((None, ATT_WIDTH, TQ), lambda b, s, qb, kt: (b, 0, qb[s])),
            pl.BlockSpec((None, ATT_WIDTH, TQ), lambda b, s, qb, kt: (b, 0, qb[s])),
            _resident((ATT_WIDTH, TQ), lambda b, s, qb, kt: (0, 0)),
            _resident((ATT_WIDTH, TQ), lambda b, s, qb, kt: (0, 0)),
            pl.BlockSpec((None, IDX_HEADS, TQ), lambda b, s, qb, kt: (b, 0, qb[s])),
            pl.BlockSpec((S, LANES), lambda b, s, qb, kt: (b, 0), pipeline_mode=pl.Buffered(1)),
            pl.BlockSpec((TK, ATT_WIDTH), lambda b, s, qb, kt: (b * nk + kt[s], 1)),
            pl.BlockSpec((None, ATT_HEADS * V_ROWS, TK), lambda b, s, qb, kt: (b, 0, kt[s])),
            _resident((2, ATT_HEADS, TQ, TQ), lambda b, s, qb, kt: (0, 0, 0, 0)),
        ],
        out_specs=pl.BlockSpec((TQ, ATT_WIDTH), lambda b, s, qb, kt: (b * nq + qb[s], 0)),
        scratch_shapes=[
            pltpu.VMEM((S, TQ), jnp.int32),
            pltpu.VMEM((S, TQ), jnp.int16),
            pltpu.VMEM((ATT_WIDTH, TQ), BF16),
            pltpu.VMEM((ATT_WIDTH, TQ), BF16),
            pltpu.VMEM((8, TQ), jnp.int32),
            pltpu.VMEM((ATT_HEADS, TQ), F32),
            pltpu.VMEM((ATT_HEADS * V_ROWS, TQ), F32),
            pltpu.VMEM((wide, TQ), F32),
            pltpu.VMEM((3, wide, TQ), F32),
            pltpu.VMEM((2, wide, TQ), BF16),
        ],
    )
    return pl.pallas_call(
        kern,
        grid_spec=grid_spec,
        out_shape=jax.ShapeDtypeStruct((B * S, ATT_WIDTH), BF16),
        compiler_params=_cparams(("parallel", "arbitrary")),
        name="sparse_attn",
    )(qb_tab, kt_tab, qt, qit, em, om, wit, kk, z, vt, tab)


def _mix_kernel(x_ref, att_ref, gb_ref, gc_ref, u_ref, gch_ref, uh_ref, ga_ref, gv_ref,
                cw_ref, wa_ref, wb_ref, wo_ref, ling_ref, linb_ref, l1g_ref, l1b_ref,
                wrh_ref, wrl_ref, br_ref, h1_ref, h1b_ref, te_ref, tg_ref, *, tm, S, HALO, n_exp):
    i = pl.program_id(0)
    seq_start = (i * tm) % S == 0
    cu = gc_ref[...].astype(F32) * u_ref[...].astype(F32)
    cuh = gch_ref[...].astype(F32) * uh_ref[...].astype(F32)
    cuh = jnp.where(seq_start, 0.0, cuh)
    ext = jnp.concatenate([cuh, cu], axis=0)
    y = (cw_ref[2:3, :] * cu
         + cw_ref[1:2, :] * ext[HALO - 1:HALO - 1 + tm]
         + cw_ref[0:1, :] * ext[HALO - 2:HALO - 2 + tm])
    conv = (gb_ref[...].astype(F32) * y).astype(BF16)
    a = jnp.dot(att_ref[...], wa_ref[...], preferred_element_type=F32)
    c = jnp.dot(conv, wb_ref[...], preferred_element_type=F32)
    merged = (jax.nn.sigmoid(ga_ref[...].astype(F32)) * a
              + jax.nn.sigmoid(gv_ref[...].astype(F32)) * c).astype(BF16)
    mix = jnp.dot(merged, wo_ref[...], preferred_element_type=F32)
    h = _ln(x_ref[...], ling_ref[...], linb_ref[...])
    h1 = _ln(DEEPNORM_ALPHA * h + mix, l1g_ref[...], l1b_ref[...])
    h1_ref[...] = h1
    h1b_ref[...] = h1.astype(BF16)

    h1_hi = h1.astype(BF16)
    h1_lo = (h1 - h1_hi.astype(F32)).astype(BF16)
    logits = (jnp.dot(h1_hi, wrh_ref[...], preferred_element_type=F32)
              + (jnp.dot(h1_hi, wrl_ref[...], preferred_element_type=F32)
                 + jnp.dot(h1_lo, wrh_ref[...], preferred_element_type=F32))) + br_ref[...]
    lane = lax.broadcasted_iota(jnp.int32, (tm, LANES), 1)
    work = jnp.where(lane < n_exp, logits, -jnp.inf)
    te = jnp.zeros((tm, LANES), jnp.int32)
    tv = jnp.zeros((tm, LANES), F32)
    v0 = None
    den = jnp.zeros((tm, 1), F32)
    for k in range(TOP_K_EXPERTS):
        vk = jnp.max(work, axis=1, keepdims=True)
        ik = jnp.min(jnp.where(work == vk, lane, LANES), axis=1, keepdims=True)
        if k == 0:
            v0 = vk
        ek = jnp.exp(vk - v0)
        den = den + ek
        te = jnp.where(lane == k, ik, te)
        tv = jnp.where(lane == k, ek, tv)
        work = jnp.where(lane == ik, -jnp.inf, work)
    te_ref[...] = te
    tg_ref[...] = tv / den


def _mix(x2, att, z, conv_w, wa, wb, wo, ling, linb, l1g, l1b, wr_hi, wr_lo, br, S, tm, n_exp):
    M, D = x2.shape
    C = CONV_WIDTH
    HALO = 16
    hb = tm // HALO
    ga_blk = 4 * C // D
    gb_blk = (4 * C + 2 * D) // C
    kern = functools.partial(_mix_kernel, tm=tm, S=S, HALO=HALO, n_exp=n_exp)
    halo_map = lambda col: (lambda i: (jnp.maximum(i * hb - 1, 0), col))
    return pl.pallas_call(
        kern,
        grid=(M // tm,),
        in_specs=[
            pl.BlockSpec((tm, D), lambda i: (i, 0)),
            pl.BlockSpec((tm, ATT_WIDTH), lambda i: (i, 0)),
            pl.BlockSpec((tm, C), lambda i: (i, gb_blk)),
            pl.BlockSpec((tm, C), lambda i: (i, gb_blk + 1)),
            pl.BlockSpec((tm, C), lambda i: (i, gb_blk + 2)),
            pl.BlockSpec((HALO, C), halo_map(gb_blk + 1)),
            pl.BlockSpec((HALO, C), halo_map(gb_blk + 2)),
            pl.BlockSpec((tm, D), lambda i: (i, ga_blk)),
            pl.BlockSpec((tm, D), lambda i: (i, ga_blk + 1)),
            _resident((CONV_K, C), lambda i: (0, 0)),
            _resident((ATT_WIDTH, D), lambda i: (0, 0)),
            _resident((C, D), lambda i: (0, 0)),
            _resident((D, D), lambda i: (0, 0)),
            _resident((1, D), lambda i: (0, 0)),
            _resident((1, D), lambda i: (0, 0)),
            _resident((1, D), lambda i: (0, 0)),
            _resident((1, D), lambda i: (0, 0)),
            _resident((D, LANES), lambda i: (0, 0)),
            _resident((D, LANES), lambda i: (0, 0)),
            _resident((1, LANES), lambda i: (0, 0)),
        ],
        out_specs=[
            pl.BlockSpec((tm, D), lambda i: (i, 0)),
            pl.BlockSpec((tm, D), lambda i: (i, 0)),
            pl.BlockSpec((tm, LANES), lambda i: (i, 0)),
            pl.BlockSpec((tm, LANES), lambda i: (i, 0)),
        ],
        out_shape=[
            jax.ShapeDtypeStruct((M, D), F32),
            jax.ShapeDtypeStruct((M, D), BF16),
            jax.ShapeDtypeStruct((M, LANES), jnp.int32),
            jax.ShapeDtypeStruct((M, LANES), F32),
        ],
        compiler_params=_cparams(("parallel",)),
        name="mix_ln1",
    )(x2, att, z, z, z, z, z, z, z, conv_w, wa, wb, wo, ling, linb, l1g, l1b, wr_hi, wr_lo, br)


def _moe_kernel(be_ref, bv_ref, bf_ref, xs_ref, rw_ref, wg_ref, bg_ref, wu_ref, bu_ref, wd_ref, bd_ref,
                o_ref, acc_sc, wg_sc, wu_sc, wd_sc):
    i = pl.program_id(0)
    f = pl.program_id(1)
    last_f = pl.num_programs(1) - 1
    tf = wg_ref.shape[1]
    used = bv_ref[i] > 0

    @pl.when(bf_ref[i] > 0)
    def _():
        wg_sc[f] = wg_ref[...].astype(BF16)
        wu_sc[f] = wu_ref[...].astype(BF16)
        wd_sc[f] = wd_ref[...].astype(BF16)

    @pl.when(used)
    def _():
        @pl.when(f == 0)
        def _():
            acc_sc[...] = jnp.zeros(acc_sc.shape, F32)

        xb = xs_ref[...]
        cols = pl.ds(pl.multiple_of(f * tf, tf), tf)
        g = jnp.dot(xb, wg_sc[f], preferred_element_type=F32) + bg_ref[:, cols]
        u = jnp.dot(xb, wu_sc[f], preferred_element_type=F32) + bu_ref[:, cols]
        g = jnp.minimum(g, SWIGLU_LIMIT)
        u = jnp.clip(u, -SWIGLU_LIMIT, SWIGLU_LIMIT)
        a = (u + 1.0) * (g * jax.nn.sigmoid(SWIGLU_ALPHA * g))
        acc_sc[...] += jnp.dot(a.astype(BF16), wd_sc[f], preferred_element_type=F32)

        @pl.when(f == last_f)
        def _():
            o_ref[...] = ((acc_sc[...] + bd_ref[...]) * rw_ref[:, 0:1]).astype(o_ref.dtype)

    @pl.when(jnp.logical_not(used) & (f == last_f))
    def _():
        o_ref[...] = jnp.zeros(o_ref.shape, o_ref.dtype)


def _moe(block_e, block_used, block_first, xs, rw, wg, bg, wu, bu, wd, bd, tm, tf):
    R, D = xs.shape
    F = wg.shape[2]
    nf = F // tf

    def f_blk(i, f, bf):
        return jnp.where(bf[i] > 0, f, nf - 1)

    grid_spec = pltpu.PrefetchScalarGridSpec(
        num_scalar_prefetch=3,
        grid=(R // tm, nf),
        in_specs=[
            pl.BlockSpec((tm, D), lambda i, f, be, bv, bf: (i, 0)),
            pl.BlockSpec((tm, LANES), lambda i, f, be, bv, bf: (i, 0)),
            pl.BlockSpec((None, D, tf), lambda i, f, be, bv, bf: (be[i], 0, f_blk(i, f, bf))),
            pl.BlockSpec((None, 1, F), lambda i, f, be, bv, bf: (be[i], 0, 0)),
            pl.BlockSpec((None, D, tf), lambda i, f, be, bv, bf: (be[i], 0, f_blk(i, f, bf))),
            pl.BlockSpec((None, 1, F), lambda i, f, be, bv, bf: (be[i], 0, 0)),
            pl.BlockSpec((None, tf, D), lambda i, f, be, bv, bf: (be[i], f_blk(i, f, bf), 0)),
            pl.BlockSpec((None, 1, D), lambda i, f, be, bv, bf: (be[i], 0, 0)),
        ],
        out_specs=pl.BlockSpec((tm, D), lambda i, f, be, bv, bf: (i, 0)),
        scratch_shapes=[
            pltpu.VMEM((tm, D), F32),
            pltpu.VMEM((nf, D, tf), BF16),
            pltpu.VMEM((nf, D, tf), BF16),
            pltpu.VMEM((nf, tf, D), BF16),
        ],
    )
    return pl.pallas_call(
        _moe_kernel,
        grid_spec=grid_spec,
        out_shape=jax.ShapeDtypeStruct((R, D), BF16),
        compiler_params=_cparams(("arbitrary", "arbitrary")),
        name="moe_experts",
    )(block_e, block_used, block_first, xs, rw, wg, bg, wu, bu, wd, bd)


def _ple_kernel(h1_ref, h1b_ref, ff_ref, p_ref, wpg_ref, wpp_ref, g_ref, b_ref, o_ref):
    h1 = h1_ref[...]
    ff = ff_ref[0].astype(F32)
    for k in range(1, TOP_K_EXPERTS):
        ff = ff + ff_ref[k].astype(F32)
    gate = jax.nn.sigmoid(jnp.dot(h1b_ref[...], wpg_ref[...], preferred_element_type=F32))
    proj = jnp.dot(p_ref[...].astype(BF16), wpp_ref[...], preferred_element_type=F32)
    o_ref[...] = _ln(DEEPNORM_ALPHA * h1 + ff + gate * proj, g_ref[...], b_ref[...])


def _ple(h1, h1b, ffg, p2, wpg, wpp, g, b, tm):
    M, D = h1.shape
    P = p2.shape[1]
    return pl.pallas_call(
        _ple_kernel,
        grid=(M // tm,),
        in_specs=[
            pl.BlockSpec((tm, D), lambda i: (i, 0)),
            pl.BlockSpec((tm, D), lambda i: (i, 0)),
            pl.BlockSpec((TOP_K_EXPERTS, tm, D), lambda i: (0, i, 0)),
            pl.BlockSpec((tm, P), lambda i: (i, 0)),
            _resident((D, D), lambda i: (0, 0)),
            _resident((P, D), lambda i: (0, 0)),
            _resident((1, D), lambda i: (0, 0)),
            _resident((1, D), lambda i: (0, 0)),
        ],
        out_specs=pl.BlockSpec((tm, D), lambda i: (i, 0)),
        out_shape=jax.ShapeDtypeStruct((M, D), F32),
        compiler_params=_cparams(("parallel",)),
        name="ple_ln2",
    )(h1, h1b, ffg, p2, wpg, wpp, g, b)


def _pick(n, prefs):
    for t in prefs:
        if n % t == 0:
            return t
    return n


def kernel(x, p, ln_in_g, ln_in_b, rel_bias, w_in, conv_w, w_att_br, w_conv_br, w_o, ln1_g, ln1_b,
           w_router, b_router, w_gate, b_gate, w_up, b_up, w_down, b_down, w_ple_gate, w_ple_proj,
           ln2_g, ln2_b):
    B, S, D = x.shape
    M = B * S
    E = w_router.shape[-1]
    assert w_in.shape[0] == DEPTH == 1
    C = CONV_WIDTH
    x2 = x.reshape(M, D)
    row = lambda v: v.reshape(1, -1).astype(F32)

    w = w_in[0]
    sp = [ATT_WIDTH, ATT_WIDTH, ATT_WIDTH, IDX_HEADS * IDX_DIM, IDX_DIM, IDX_HEADS, C, C, C, D, D]
    offs = [0]
    for s_ in sp:
        offs.append(offs[-1] + s_)
    seg = lambda a, b_: w[:, offs[a]:offs[b_]]
    assert (2 * D) % C == 0 and ATT_WIDTH == C and IDX_HEADS * IDX_DIM == C
    w_q = seg(0, 1) * (ATT_HEAD_DIM ** -0.5 * LOG2E)
    w_main = jnp.concatenate([w_q, seg(1, 4), seg(9, 11), seg(6, 9)], axis=1)
    n_main = w_main.shape[1]
    tn = _pick(n_main, (1024, 512, 256, 128))
    w_main = w_main.astype(BF16)
    w_small = jnp.pad(seg(4, 6), ((0, 0), (0, LANES - IDX_DIM - IDX_HEADS))).astype(BF16)

    tm_in = _pick(M, (1024, 512, 256, 128))
    z, zs = _inproj(x2, row(ln_in_g), row(ln_in_b), w_main, w_small, tm_in, tn)

    TQ = _pick(S, (256, 128))
    TK = _pick(S, (1024, 512, 256))
    CK = _pick(S, (512, 256))
    ki = zs[:, :IDX_DIM].astype(BF16)
    kk = jnp.concatenate([ki, ki], axis=1)
    wit = jnp.swapaxes(zs[:, IDX_DIM:IDX_DIM + IDX_HEADS].reshape(B, S, IDX_HEADS), 1, 2)
    col_t = lambda c: jnp.swapaxes(z[:, c * ATT_WIDTH:(c + 1) * ATT_WIDTH].reshape(B, S, ATT_WIDTH), 1, 2)
    qt, vt, qit = col_t(0), col_t(2), col_t(3)
    vt = jnp.concatenate([vt.reshape(B, ATT_HEADS, ATT_HEAD_DIM, S),
                          jnp.ones((B, ATT_HEADS, V_PAD, S), BF16)], axis=2).reshape(B, ATT_HEADS * V_ROWS, S)
    tab = _bias_tab(rel_bias.astype(F32), TQ)
    att = _sparse_attn(z, qt, qit, vt, kk, wit, tab, B, S, TQ, TK, CK)

    wr = jnp.pad(w_router[0].astype(F32), ((0, 0), (0, LANES - E)))
    wr_hi = wr.astype(BF16)
    wr_lo = (wr - wr_hi.astype(F32)).astype(BF16)
    br = jnp.pad(b_router[0].astype(F32), (0, LANES - E)).reshape(1, LANES)
    tm_mix = _pick(M, (256, 128))
    h1, h1b, te, tg = _mix(x2, att, z, conv_w[0].astype(F32), w_att_br[0].astype(BF16),
                           w_conv_br[0].astype(BF16), w_o[0].astype(BF16),
                           row(ln_in_g), row(ln_in_b), row(ln1_g[0]), row(ln1_b[0]), wr_hi, wr_lo, br,
                           S, tm_mix, E)

    top_e = te[:, :TOP_K_EXPERTS]
    gates = tg[:, :TOP_K_EXPERTS]
    n_assign = M * TOP_K_EXPERTS
    tm_e = _pick(n_assign, (512, 256, 128))
    e_flat = top_e.reshape(-1)
    order = jnp.argsort(e_flat, stable=True).astype(jnp.int32)
    onehot = (e_flat[:, None] == jnp.arange(E, dtype=jnp.int32)[None, :]).astype(jnp.int32)
    csum = jnp.cumsum(onehot, axis=0)
    counts = csum[-1]
    rank = jnp.take_along_axis(csum, e_flat[:, None], axis=1)[:, 0] - 1
    starts = jnp.cumsum(counts) - counts
    padded = (counts + tm_e - 1) // tm_e * tm_e
    pends = jnp.cumsum(padded)
    pstarts = pends - padded
    pos = pstarts[e_flat] + rank
    n_blocks = (n_assign + E * (tm_e - 1) + tm_e - 1) // tm_e
    n_rows = n_blocks * tm_e
    r = jnp.arange(n_rows, dtype=jnp.int32)
    row_e = jnp.minimum(jnp.sum((r[:, None] >= pends[None, :]).astype(jnp.int32), axis=1), E - 1)
    j = r - pstarts[row_e]
    valid = j < counts[row_e]
    src = jnp.clip(starts[row_e] + j, 0, n_assign - 1)
    a_id = order[src]
    row_tok = jnp.where(valid, a_id // TOP_K_EXPERTS, r % M)
    row_w = jnp.where(valid, gates.reshape(-1)[a_id], 0.0)
    n_used = pends[-1] // tm_e
    blk = jnp.arange(n_blocks, dtype=jnp.int32)
    block_used = (blk < n_used).astype(jnp.int32)
    block_e = row_e.reshape(n_blocks, tm_e)[:, 0][jnp.minimum(blk, n_used - 1)]

    xs = h1b[row_tok]
    rw = jnp.broadcast_to(row_w[:, None], (n_rows, LANES))
    F = w_gate.shape[-1]
    tf = _pick(F, (256, 128))
    block_first = jnp.concatenate([jnp.ones((1,), jnp.int32),
                                   (block_e[1:] != block_e[:-1]).astype(jnp.int32)]) * block_used
    eo = _moe(block_e, block_used, block_first, xs, rw,
              w_gate[0].astype(F32), b_gate[0].reshape(E, 1, F).astype(F32),
              w_up[0].astype(F32), b_up[0].reshape(E, 1, F).astype(F32),
              w_down[0].astype(F32), b_down[0].reshape(E, 1, D).astype(F32), tm_e, tf)

    ffg = eo[pos.reshape(M, TOP_K_EXPERTS).T]

    tm_p = _pick(M, (256, 128))
    out = _ple(h1, h1b, ffg, p[0].reshape(M, -1), w_ple_gate[0].astype(BF16),
               w_ple_proj[0].astype(BF16), row(ln2_g[0]), row(ln2_b[0]), tm_p)
    return out.reshape(B, S, D)
```

```python
import functools
import math

import jax
import jax.numpy as jnp
from jax import lax
from jax.experimental import pallas as pl
from jax.experimental.pallas import tpu as pltpu

CHUNK = 64
CHUNK_SHIFT = 6
assert 1 << CHUNK_SHIFT == CHUNK
ATT_HEADS = 8
ATT_HEAD_DIM = 128
ATT_WIDTH = ATT_HEADS * ATT_HEAD_DIM
IDX_HEADS = 16
IDX_DIM = 64
MAX_TOPK = 256
CONV_WIDTH = 1024
CONV_K = 3
REL_BUCKETS = 32
REL_MAX_DIST = 128
TOP_K_EXPERTS = 4
SWIGLU_LIMIT = 7.0
SWIGLU_ALPHA = 1.702
LN_EPS = 1e-5
DEPTH = 1
DEEPNORM_ALPHA = (2 * DEPTH) ** 0.25

LANES = 128
BF16_ROWS = 16
V_PAD = BF16_ROWS
V_ROWS = ATT_HEAD_DIM + V_PAD
UNCHECKED_PASSES = 24
WIDE_SUBTILES = 4
VMEM_LIMIT_BYTES = 56 * 1024 * 1024

LOG2E = math.log2(math.e)
INT_MIN = -(2 ** 31)
KEY_NEG_INF = (0xFF800000 ^ 0x7FFFFFFF) - (1 << 32)

F32 = jnp.float32
BF16 = jnp.bfloat16


def _cparams(sem):
    return pltpu.CompilerParams(dimension_semantics=sem, vmem_limit_bytes=VMEM_LIMIT_BYTES)


def _ln(xf, g, b):
    mu = jnp.mean(xf, axis=-1, keepdims=True)
    xc = xf - mu
    var = jnp.mean(xc * xc, axis=-1, keepdims=True)
    return xc * lax.rsqrt(var + LN_EPS) * g + b


def _rep_lanes(v, rep):
    return v if rep == 1 else jnp.concatenate([v] * rep, axis=1)


def _resident(shape, index_map):
    return pl.BlockSpec(shape, index_map, pipeline_mode=pl.Buffered(1))


def _bias_tab_kernel(rb_ref, o_ref, *, T):
    c = pl.program_id(0)
    h = pl.program_id(1)
    half = REL_BUCKETS // 2
    max_exact = half // 2
    j = lax.broadcasted_iota(jnp.int32, (T, T), 0)
    i = lax.broadcasted_iota(jnp.int32, (T, T), 1)
    rel = j - i - c * T
    ret = jnp.where(rel > 0, half, 0)
    n = jnp.abs(rel)
    nf = jnp.maximum(n, 1).astype(F32)
    large = max_exact + (jnp.log(nf / max_exact) / math.log(REL_MAX_DIST / max_exact)
                         * (half - max_exact)).astype(jnp.int32)
    large = jnp.minimum(large, half - 1)
    bucket = ret + jnp.where(n < max_exact, n, large)
    far = rb_ref[half - 1, h]
    acc = jnp.zeros((T, T), F32)
    for b in range(REL_BUCKETS):
        acc = jnp.where(bucket == b, rb_ref[b, h] - far, acc)
    o_ref[...] = acc * LOG2E


def _bias_tab(rel_bias, T):
    return pl.pallas_call(
        functools.partial(_bias_tab_kernel, T=T),
        grid=(2, ATT_HEADS),
        in_specs=[pl.BlockSpec(memory_space=pltpu.SMEM)],
        out_specs=pl.BlockSpec((None, None, T, T), lambda c, h: (c, h, 0, 0)),
        out_shape=jax.ShapeDtypeStruct((2, ATT_HEADS, T, T), F32),
        compiler_params=_cparams(("arbitrary", "arbitrary")),
        name="bias_tab",
    )(rel_bias)


def _inproj_kernel(x_ref, g_ref, b_ref, w_ref, ws_ref, z_ref, zs_ref, hn_sc):
    j = pl.program_id(1)

    @pl.when(j == 0)
    def _():
        hn = _ln(x_ref[...], g_ref[...], b_ref[...]).astype(BF16)
        hn_sc[...] = hn
        zs_ref[...] = jnp.dot(hn, ws_ref[...], preferred_element_type=F32)

    z_ref[...] = jnp.dot(hn_sc[...], w_ref[...], preferred_element_type=F32).astype(BF16)


def _inproj(x2, g, b, w_main, w_small, tm, tn):
    M, D = x2.shape
    N = w_main.shape[1]
    return pl.pallas_call(
        _inproj_kernel,
        grid=(M // tm, N // tn),
        in_specs=[
            pl.BlockSpec((tm, D), lambda i, j: (i, 0)),
            pl.BlockSpec((1, D), lambda i, j: (0, 0)),
            pl.BlockSpec((1, D), lambda i, j: (0, 0)),
            pl.BlockSpec((D, tn), lambda i, j: (0, j)),
            pl.BlockSpec((D, LANES), lambda i, j: (0, 0)),
        ],
        out_specs=[
            pl.BlockSpec((tm, tn), lambda i, j: (i, j)),
            pl.BlockSpec((tm, LANES), lambda i, j: (i, 0)),
        ],
        out_shape=[
            jax.ShapeDtypeStruct((M, N), BF16),
            jax.ShapeDtypeStruct((M, LANES), F32),
        ],
        scratch_shapes=[pltpu.VMEM((tm, D), BF16)],
        compiler_params=_cparams(("parallel", "arbitrary")),
        name="ln_inproj",
    )(x2, g, b, w_main, w_small)


def _attn_kernel(qb_ref, kt_ref, qt_ref, qit_ref, em_ref, om_ref, wi_ref, kk_ref, k_ref, vt_ref, tab_ref,
                 o_ref, key_sc, key16_sc, qe_sc, qo_sc, thr_sc, m_sc, acc_sc, madd_sc, x_sc, p_sc,
                 *, TQ, TK, CK, S, topk):
    qb = qb_ref[pl.program_id(1)]
    kt = kt_ref[pl.program_id(1)]
    t0 = qb * TQ
    lim_max = t0 + TQ
    last_kt = (lim_max - 1) // TK
    n_sub = TK // TQ
    SLAB = 32
    SLAB16 = 2 * SLAB

    @pl.when(kt == 0)
    def _phase1():
        qi = qit_ref[...]
        qe_sc[...] = qi * em_ref[...]
        qo_sc[...] = qi * om_ref[...]
        n_ck = (lim_max + CK - 1) // CK
        q_idx = t0 + lax.broadcasted_iota(jnp.int32, (1, TQ), 1)
        limit = ((q_idx >> CHUNK_SHIFT) + 1) << CHUNK_SHIFT

        def score_chunk(c, carry):
            c0 = pl.multiple_of(c * CK, CK)
            kk = kk_ref[pl.ds(c0, CK), :]
            acc = jnp.zeros((CK, TQ), F32)
            for jp in range(IDX_HEADS // 2):
                sl = slice(LANES * jp, LANES * (jp + 1))
                le = jnp.dot(kk, qe_sc[sl, :], preferred_element_type=F32)
                lo = jnp.dot(kk, qo_sc[sl, :], preferred_element_type=F32)
                acc = acc + jnp.maximum(le, 0.0) * wi_ref[2 * jp:2 * jp + 1, :]
                acc = acc + jnp.maximum(lo, 0.0) * wi_ref[2 * jp + 1:2 * jp + 2, :]
            s_idx = c0 + lax.broadcasted_iota(jnp.int32, (CK, TQ), 0)
            sc = jnp.where(s_idx < limit, acc, -jnp.inf)
            bits = pltpu.bitcast(sc, jnp.int32)
            key = bits ^ ((bits >> 31) & 0x7FFFFFFF)
            key_sc[pl.ds(c0, CK), :] = key
            key16_sc[pl.ds(c0, CK), :] = (key >> 16).astype(jnp.int16)
            return carry

        lax.fori_loop(0, n_ck, score_chunk, 0)

        def count(pred):
            def chunk(c, acc):
                c0 = pl.multiple_of(c * CK, CK)
                for r in range(CK // SLAB):
                    blk = key_sc[pl.ds(c0 + SLAB * r, SLAB), :]
                    acc = acc + pred(blk, c0 + SLAB * r).astype(jnp.int32)
                return acc

            acc = lax.fori_loop(0, n_ck, chunk, jnp.zeros((SLAB, TQ), jnp.int32))
            return jnp.sum(acc, axis=0, keepdims=True)

        def bit_pass(i, state, low):
            u, c_u, settled = state
            cand_u = u | jnp.left_shift(jnp.int32(1), 31 - i)
            cand_h = ((cand_u & 0xFFFF) - 0x8000) if low else ((cand_u ^ INT_MIN) >> 16)
            cand16 = jnp.broadcast_to(cand_h.astype(jnp.int16), (SLAB16, TQ))

            def chunk(c, acc):
                c0 = pl.multiple_of(c * CK, CK)
                for r in range(CK // SLAB16):
                    blk = key16_sc[pl.ds(c0 + SLAB16 * r, SLAB16), :]
                    acc = acc + (blk >= cand16).astype(jnp.int16)
                return acc

            acc = lax.fori_loop(0, n_ck, chunk, jnp.zeros((SLAB16, TQ), jnp.int16))
            cnt = jnp.sum(acc.astype(jnp.int32), axis=0, keepdims=True)
            take = cnt >= topk
            settled = jnp.where(cnt == topk, 1, settled)
            return jnp.where(take, cand_u, u), jnp.where(take, cnt, c_u), settled

        settled0 = (limit <= topk).astype(jnp.int32)
        zero = jnp.zeros((1, TQ), jnp.int32)
        state = lax.fori_loop(0, 16, functools.partial(bit_pass, low=False), (zero, zero + topk, settled0))

        t_hi = jnp.broadcast_to((state[0] ^ INT_MIN) >> 16, (SLAB16, TQ))

        def repack_chunk(c, carry):
            c0 = pl.multiple_of(c * CK, CK)
            for r in range(CK // SLAB16):
                rows = pl.ds(c0 + SLAB16 * r, SLAB16)
                key = key_sc[rows, :]
                k_hi = key >> 16
                low16 = jnp.where(k_hi > t_hi, 0x7FFF, jnp.where(k_hi < t_hi, -0x8000, (key & 0xFFFF) - 0x8000))
                key16_sc[rows, :] = low16.astype(jnp.int16)
            return carry

        lax.fori_loop(0, n_ck, repack_chunk, 0)

        low_pass = functools.partial(bit_pass, low=True)

        def checked_pass(s):
            nxt = low_pass(s[0], s[1:4])
            return (s[0] + 1,) + nxt + (jnp.min(nxt[2]),)

        state = lax.fori_loop(16, UNCHECKED_PASSES, low_pass, state)
        state = (jnp.int32(UNCHECKED_PASSES),) + state + (jnp.min(state[2]),)
        _, u, c_u, _, _ = lax.while_loop(lambda s: (s[0] < 32) & (s[4] == 0), checked_pass, state)
        t_star = u ^ INT_MIN
        thr_sc[...] = jnp.broadcast_to(jnp.maximum(t_star, KEY_NEG_INF + 1), thr_sc.shape)

        tie = (c_u > topk) & (t_star > KEY_NEG_INF)

        @pl.when(jnp.max(tie.astype(jnp.int32)) > 0)
        def _ties():
            t_b = jnp.broadcast_to(t_star, (SLAB, TQ))
            need = topk - count(lambda blk, _: blk > t_b)
            row = lax.broadcasted_iota(jnp.int32, (SLAB, TQ), 0)

            def idx_step(i, v):
                cand_v = v | jnp.left_shift(jnp.int32(1), S.bit_length() - 1 - i)
                cand_b = jnp.broadcast_to(cand_v, (SLAB, TQ))
                cnt = count(lambda blk, s0: (blk == t_b) & (row + s0 < cand_b))
                return jnp.where(cnt < need, cand_v, v)

            v = lax.fori_loop(0, S.bit_length(), idx_step, zero)
            drop_from = jnp.broadcast_to(jnp.where(tie, v + 1, S), (SLAB, TQ))

            def fix_chunk(c, carry):
                c0 = pl.multiple_of(c * CK, CK)
                for r in range(CK // SLAB):
                    rows = pl.ds(c0 + SLAB * r, SLAB)
                    blk = key_sc[rows, :]
                    drop = (blk == t_b) & (row + (c0 + SLAB * r) >= drop_from)
                    key_sc[rows, :] = jnp.where(drop, blk - 1, blk)
                return carry

            lax.fori_loop(0, n_ck, fix_chunk, 0)

        m_sc[...] = jnp.full(m_sc.shape, -1e30, F32)
        acc_sc[...] = jnp.zeros(acc_sc.shape, F32)

    def attend(s0, r0, bias_idx, nk=TQ):
        madd_sc[0:nk, :] = jnp.where(key_sc[pl.ds(s0, nk), :] >= thr_sc[0:1, :], 0.0, -jnp.inf)

        def logits(h):
            hs = slice(ATT_HEAD_DIM * h, ATT_HEAD_DIM * (h + 1))
            x = jnp.dot(k_ref[pl.ds(r0, nk), hs], qt_ref[hs, :],
                        preferred_element_type=F32) + madd_sc[0:nk, :]
            if bias_idx is not None:
                x = x + tab_ref[bias_idx, h]
            x_sc[h % 3, 0:nk, :] = x
            return jnp.max(x, axis=0, keepdims=True)

        def weighted_values(h, alpha):
            hv = slice(V_ROWS * h, V_ROWS * (h + 1))
            pv = jnp.dot(vt_ref[hv, pl.ds(r0, nk)], p_sc[h % 2, 0:nk, :], preferred_element_type=F32)
            acc_sc[hv, :] = acc_sc[hv, :] * alpha + pv

        m_cur = {0: logits(0), 1: logits(1)}
        alphas = {}
        for h in range(ATT_HEADS):
            if h + 2 < ATT_HEADS:
                m_cur[h + 2] = logits(h + 2)
            if h >= 1:
                weighted_values(h - 1, alphas.pop(h - 1))
            m_old = m_sc[h:h + 1, :]
            m_new = jnp.maximum(m_old, m_cur.pop(h))
            alpha = jnp.exp2(m_old - m_new)
            p_sc[h % 2, 0:nk, :] = jnp.exp2(x_sc[h % 3, 0:nk, :] - m_new).astype(BF16)
            m_sc[h:h + 1, :] = m_new
            alphas[h] = alpha
        weighted_values(ATT_HEADS - 1, alphas.pop(ATT_HEADS - 1))

    wide = min(WIDE_SUBTILES * TQ, TK)
    all_far = (kt + 1) * TK <= t0 - TQ

    @pl.when(all_far)
    def _phase2_far():
        def sub(c, carry):
            r0 = pl.multiple_of(c * wide, wide)
            attend(pl.multiple_of(kt * TK + r0, wide), r0, None, wide)
            return carry

        lax.fori_loop(0, TK // wide, sub, 0)

    @pl.when(jnp.logical_not(all_far))
    def _phase2():
        def sub(c, carry):
            r0 = pl.multiple_of(c * TQ, TQ)
            s0 = pl.multiple_of(kt * TK + r0, TQ)
            d = s0 - t0

            @pl.when(d == 0)
            def _():
                attend(s0, r0, 0)

            @pl.when(d == -TQ)
            def _():
                attend(s0, r0, 1)

            @pl.when(d < -TQ)
            def _():
                attend(s0, r0, None)

            return carry

        lax.fori_loop(0, n_sub, sub, 0)

    @pl.when(kt == last_kt)
    def _finish():
        for h in range(ATT_HEADS):
            hs = slice(ATT_HEAD_DIM * h, ATT_HEAD_DIM * (h + 1))
            num = acc_sc[V_ROWS * h:V_ROWS * h + ATT_HEAD_DIM, :]
            den = acc_sc[V_ROWS * h + ATT_HEAD_DIM:V_ROWS * h + ATT_HEAD_DIM + 1, :]
            o_ref[:, hs] = (num / den).T.astype(o_ref.dtype)


def _sparse_attn(z, qt, qit, vt, kk, wit, tab, B, S, TQ, TK, CK):
    nq = S // TQ
    nk = S // TK
    topk = min(MAX_TOPK, S // 4)
    wide = min(WIDE_SUBTILES * TQ, TK)
    feat =jnp.arange(ATT_WIDTH, dtype=jnp.int32)[:, None]
    em = jnp.broadcast_to(((feat // IDX_DIM) % 2 == 0).astype(BF16), (ATT_WIDTH, TQ))
    om = jnp.broadcast_to(((feat // IDX_DIM) % 2 == 1).astype(BF16), (ATT_WIDTH, TQ))

    pairs = [(i, k) for i in range(nq) for k in range(((i + 1) * TQ - 1) // TK + 1)]
    qb_tab = jnp.asarray([i for i, _ in pairs], jnp.int32)
    kt_tab = jnp.asarray([k for _, k in pairs], jnp.int32)

    kern = functools.partial(_attn_kernel, TQ=TQ, TK=TK, CK=CK, S=S, topk=topk)
    grid_spec = pltpu.PrefetchScalarGridSpec(
        num_scalar_prefetch=2,
        grid=(B, len(pairs)),
        in_specs=[
            pl.BlockSpec((None, ATT_WIDTH, TQ), lambda b, s, qb, kt: (b, 0, qb[s])),
            pl.BlockSpec((None, ATT_WIDTH, TQ), lambda b, s, qb, kt: (b, 0, qb[s])),
            _resident((ATT_WIDTH, TQ), lambda b, s, qb, kt: (0, 0)),
            _resident((ATT_WIDTH, TQ), lambda b, s, qb, kt: (0, 0)),
            pl.BlockSpec((None, IDX_HEADS, TQ), lambda b, s, qb, kt: (b, 0, qb[s])),
            pl.BlockSpec((S, LANES), lambda b, s, qb, kt: (b, 0), pipeline_mode=pl.Buffered(1)),
            pl.BlockSpec((TK, ATT_WIDTH), lambda b, s, qb, kt: (b * nk + kt[s], 1)),
            pl.BlockSpec((None, ATT_HEADS * V_ROWS, TK), lambda b, s, qb, kt: (b, 0, kt[s])),
            _resident((2, ATT_HEADS, TQ, TQ), lambda b, s, qb, kt: (0, 0, 0, 0)),
        ],
        out_specs=pl.BlockSpec((TQ, ATT_WIDTH), lambda b, s, qb, kt: (b * nq + qb[s], 0)),
        scratch_shapes=[
            pltpu.VMEM((S, TQ), jnp.int32),
            pltpu.VMEM((S, TQ), jnp.int16),
            pltpu.VMEM((ATT_WIDTH, TQ), BF16),
            pltpu.VMEM((ATT_WIDTH, TQ), BF16),
            pltpu.VMEM((8, TQ), jnp.int32),
            pltpu.VMEM((ATT_HEADS, TQ), F32),
            pltpu.VMEM((ATT_HEADS * V_ROWS, TQ), F32),
            pltpu.VMEM((wide, TQ), F32),
            pltpu.VMEM((3, wide, TQ), F32),
            pltpu.VMEM((2, wide, TQ), BF16),
        ],
    )
    return pl.pallas_call(
        kern,
        grid_spec=grid_spec,
        out_shape=jax.ShapeDtypeStruct((B * S, ATT_WIDTH), BF16),
        compiler_params=_cparams(("parallel", "arbitrary")),
        name="sparse_attn",
    )(qb_tab, kt_tab, qt, qit, em, om, wit, kk, z, vt, tab)


def _mix_kernel(x_ref, att_ref, gb_ref, gc_ref, u_ref, gch_ref, uh_ref, ga_ref, gv_ref,
                cw_ref, wa_ref, wb_ref, wo_ref, ling_ref, linb_ref, l1g_ref, l1b_ref,
                wrh_ref, wrl_ref, br_ref, h1_ref, h1b_ref, te_ref, tg_ref, *, tm, S, HALO, n_exp):
    i = pl.program_id(0)
    seq_start = (i * tm) % S == 0
    cu = gc_ref[...].astype(F32) * u_ref[...].astype(F32)
    cuh = gch_ref[...].astype(F32) * uh_ref[...].astype(F32)
    cuh = jnp.where(seq_start, 0.0, cuh)
    ext = jnp.concatenate([cuh, cu], axis=0)
    y = (cw_ref[2:3, :] * cu
         + cw_ref[1:2, :] * ext[HALO - 1:HALO - 1 + tm]
         + cw_ref[0:1, :] * ext[HALO - 2:HALO - 2 + tm])
    conv = (gb_ref[...].astype(F32) * y).astype(BF16)
    a = jnp.dot(att_ref[...], wa_ref[...], preferred_element_type=F32)
    c = jnp.dot(conv, wb_ref[...], preferred_element_type=F32)
    merged = (jax.nn.sigmoid(ga_ref[...].astype(F32)) * a
              + jax.nn.sigmoid(gv_ref[...].astype(F32)) * c).astype(BF16)
    mix = jnp.dot(merged, wo_ref[...], preferred_element_type=F32)
    h = _ln(x_ref[...], ling_ref[...], linb_ref[...])
    h1 = _ln(DEEPNORM_ALPHA * h + mix, l1g_ref[...], l1b_ref[...])
    h1_ref[...] = h1
    h1b_ref[...] = h1.astype(BF16)

    h1_hi = h1.astype(BF16)
    h1_lo = (h1 - h1_hi.astype(F32)).astype(BF16)
    logits = (jnp.dot(h1_hi, wrh_ref[...], preferred_element_type=F32)
              + (jnp.dot(h1_hi, wrl_ref[...], preferred_element_type=F32)
                 + jnp.dot(h1_lo, wrh_ref[...], preferred_element_type=F32))) + br_ref[...]
    lane = lax.broadcasted_iota(jnp.int32, (tm, LANES), 1)
    work = jnp.where(lane < n_exp, logits, -jnp.inf)
    te = jnp.zeros((tm, LANES), jnp.int32)
    tv = jnp.zeros((tm, LANES), F32)
    v0 = None
    den = jnp.zeros((tm, 1), F32)
    for k in range(TOP_K_EXPERTS):
        vk = jnp.max(work, axis=1, keepdims=True)
        ik = jnp.min(jnp.where(work == vk, lane, LANES), axis=1, keepdims=True)
        if k == 0:
            v0 = vk
        ek = jnp.exp(vk - v0)
        den = den + ek
        te = jnp.where(lane == k, ik, te)
        tv = jnp.where(lane == k, ek, tv)
        work = jnp.where(lane == ik, -jnp.inf, work)
    te_ref[...] = te
    tg_ref[...] = tv / den


def _mix(x2, att, z, conv_w, wa, wb, wo, ling, linb, l1g, l1b, wr_hi, wr_lo, br, S, tm, n_exp):
    M, D = x2.shape
    C = CONV_WIDTH
    HALO = 16
    hb = tm // HALO
    ga_blk = 4 * C // D
    gb_blk = (4 * C + 2 * D) // C
    kern = functools.partial(_mix_kernel, tm=tm, S=S, HALO=HALO, n_exp=n_exp)
    halo_map = lambda col: (lambda i: (jnp.maximum(i * hb - 1, 0), col))
    return pl.pallas_call(
        kern,
        grid=(M // tm,),
        in_specs=[
            pl.BlockSpec((tm, D), lambda i: (i, 0)),
            pl.BlockSpec((tm, ATT_WIDTH), lambda i: (i, 0)),
            pl.BlockSpec((tm, C), lambda i: (i, gb_blk)),
            pl.BlockSpec((tm, C), lambda i: (i, gb_blk + 1)),
            pl.BlockSpec((tm, C), lambda i: (i, gb_blk + 2)),
            pl.BlockSpec((HALO, C), halo_map(gb_blk + 1)),
            pl.BlockSpec((HALO, C), halo_map(gb_blk + 2)),
            pl.BlockSpec((tm, D), lambda i: (i, ga_blk)),
            pl.BlockSpec((tm, D), lambda i: (i, ga_blk + 1)),
            _resident((CONV_K, C), lambda i: (0, 0)),
            _resident((ATT_WIDTH, D), lambda i: (0, 0)),
            _resident((C, D), lambda i: (0, 0)),
            _resident((D, D), lambda i: (0, 0)),
            _resident((1, D), lambda i: (0, 0)),
            _resident((1, D), lambda i: (0, 0)),
            _resident((1, D), lambda i: (0, 0)),
            _resident((1, D), lambda i: (0, 0)),
            _resident((D, LANES), lambda i: (0, 0)),
            _resident((D, LANES), lambda i: (0, 0)),
            _resident((1, LANES), lambda i: (0, 0)),
        ],
        out_specs=[
            pl.BlockSpec((tm, D), lambda i: (i, 0)),
            pl.BlockSpec((tm, D), lambda i: (i, 0)),
            pl.BlockSpec((tm, LANES), lambda i: (i, 0)),
            pl.BlockSpec((tm, LANES), lambda i: (i, 0)),
        ],
        out_shape=[
            jax.ShapeDtypeStruct((M, D), F32),
            jax.ShapeDtypeStruct((M, D), BF16),
            jax.ShapeDtypeStruct((M, LANES), jnp.int32),
            jax.ShapeDtypeStruct((M, LANES), F32),
        ],
        compiler_params=_cparams(("parallel",)),
        name="mix_ln1",
    )(x2, att, z, z, z, z, z, z, z, conv_w, wa, wb, wo, ling, linb, l1g, l1b, wr_hi, wr_lo, br)


def _moe_kernel(be_ref, bv_ref, bf_ref, xs_ref, rw_ref, wg_ref, bg_ref, wu_ref, bu_ref, wd_ref, bd_ref,
                o_ref, acc_sc, wg_sc, wu_sc, wd_sc):
    i = pl.program_id(0)
    f = pl.program_id(1)
    last_f = pl.num_programs(1) - 1
    tf = wg_ref.shape[1]
    used = bv_ref[i] > 0

    @pl.when(bf_ref[i] > 0)
    def _():
        wg_sc[f] = wg_ref[...].astype(BF16)
        wu_sc[f] = wu_ref[...].astype(BF16)
        wd_sc[f] = wd_ref[...].astype(BF16)

    @pl.when(used)
    def _():
        @pl.when(f == 0)
        def _():
            acc_sc[...] = jnp.zeros(acc_sc.shape, F32)

        xb = xs_ref[...]
        cols = pl.ds(pl.multiple_of(f * tf, tf), tf)
        g = jnp.dot(xb, wg_sc[f], preferred_element_type=F32) + bg_ref[:, cols]
        u = jnp.dot(xb, wu_sc[f], preferred_element_type=F32) + bu_ref[:, cols]
        g = jnp.minimum(g, SWIGLU_LIMIT)
        u = jnp.clip(u, -SWIGLU_LIMIT, SWIGLU_LIMIT)
        a = (u + 1.0) * (g * jax.nn.sigmoid(SWIGLU_ALPHA * g))
        acc_sc[...] += jnp.dot(a.astype(BF16), wd_sc[f], preferred_element_type=F32)

        @pl.when(f == last_f)
        def _():
            o_ref[...] = ((acc_sc[...] + bd_ref[...]) * rw_ref[:, 0:1]).astype(o_ref.dtype)

    @pl.when(jnp.logical_not(used) & (f == last_f))
    def _():
        o_ref[...] = jnp.zeros(o_ref.shape, o_ref.dtype)


def _moe(block_e, block_used, block_first, xs, rw, wg, bg, wu, bu, wd, bd, tm, tf):
    R, D = xs.shape
    F = wg.shape[2]
    nf = F // tf

    def f_blk(i, f, bf):
        return jnp.where(bf[i] > 0, f, nf - 1)

    grid_spec = pltpu.PrefetchScalarGridSpec(
        num_scalar_prefetch=3,
        grid=(R // tm, nf),
        in_specs=[
            pl.BlockSpec((tm, D), lambda i, f, be, bv, bf: (i, 0)),
            pl.BlockSpec((tm, LANES), lambda i, f, be, bv, bf: (i, 0)),
            pl.BlockSpec((None, D, tf), lambda i, f, be, bv, bf: (be[i], 0, f_blk(i, f, bf))),
            pl.BlockSpec((None, 1, F), lambda i, f, be, bv, bf: (be[i], 0, 0)),
            pl.BlockSpec((None, D, tf), lambda i, f, be, bv, bf: (be[i], 0, f_blk(i, f, bf))),
            pl.BlockSpec((None, 1, F), lambda i, f, be, bv, bf: (be[i], 0, 0)),
            pl.BlockSpec((None, tf, D), lambda i, f, be, bv, bf: (be[i], f_blk(i, f, bf), 0)),
            pl.BlockSpec((None, 1, D), lambda i, f, be, bv, bf: (be[i], 0, 0)),
        ],
        out_specs=pl.BlockSpec((tm, D), lambda i, f, be, bv, bf: (i, 0)),
        scratch_shapes=[
            pltpu.VMEM((tm, D), F32),
            pltpu.VMEM((nf, D, tf), BF16),
            pltpu.VMEM((nf, D, tf), BF16),
            pltpu.VMEM((nf, tf, D), BF16),
        ],
    )
    return pl.pallas_call(
        _moe_kernel,
        grid_spec=grid_spec,
        out_shape=jax.ShapeDtypeStruct((R, D), BF16),
        compiler_params=_cparams(("arbitrary", "arbitrary")),
        name="moe_experts",
    )(block_e, block_used, block_first, xs, rw, wg, bg, wu, bu, wd, bd)


def _ple_kernel(h1_ref, h1b_ref, ff_ref, p_ref, wpg_ref, wpp_ref, g_ref, b_ref, o_ref):
    h1 = h1_ref[...]
    ff = ff_ref[0].astype(F32)
    for k in range(1, TOP_K_EXPERTS):
        ff = ff + ff_ref[k].astype(F32)
    gate = jax.nn.sigmoid(jnp.dot(h1b_ref[...], wpg_ref[...], preferred_element_type=F32))
    proj = jnp.dot(p_ref[...].astype(BF16), wpp_ref[...], preferred_element_type=F32)
    o_ref[...] = _ln(DEEPNORM_ALPHA * h1 + ff + gate * proj, g_ref[...], b_ref[...])


def _ple(h1, h1b, ffg, p2, wpg, wpp, g, b, tm):
    M, D = h1.shape
    P = p2.shape[1]
    return pl.pallas_call(
        _ple_kernel,
        grid=(M // tm,),
        in_specs=[
            pl.BlockSpec((tm, D), lambda i: (i, 0)),
            pl.BlockSpec((tm, D), lambda i: (i, 0)),
            pl.BlockSpec((TOP_K_EXPERTS, tm, D), lambda i: (0, i, 0)),
            pl.BlockSpec((tm, P), lambda i: (i, 0)),
            _resident((D, D), lambda i: (0, 0)),
            _resident((P, D), lambda i: (0, 0)),
            _resident((1, D), lambda i: (0, 0)),
            _resident((1, D), lambda i: (0, 0)),
        ],
        out_specs=pl.BlockSpec((tm, D), lambda i: (i, 0)),
        out_shape=jax.ShapeDtypeStruct((M, D), F32),
        compiler_params=_cparams(("parallel",)),
        name="ple_ln2",
    )(h1, h1b, ffg, p2, wpg, wpp, g, b)


def _pick(n, prefs):
    for t in prefs:
        if n % t == 0:
            return t
    return n


def kernel(x, p, ln_in_g, ln_in_b, rel_bias, w_in, conv_w, w_att_br, w_conv_br, w_o, ln1_g, ln1_b,
           w_router, b_router, w_gate, b_gate, w_up, b_up, w_down, b_down, w_ple_gate, w_ple_proj,
           ln2_g, ln2_b):
    B, S, D = x.shape
    M = B * S
    E = w_router.shape[-1]
    assert w_in.shape[0] == DEPTH == 1
    C = CONV_WIDTH
    x2 = x.reshape(M, D)
    row = lambda v: v.reshape(1, -1).astype(F32)

    w = w_in[0]
    sp = [ATT_WIDTH, ATT_WIDTH, ATT_WIDTH, IDX_HEADS * IDX_DIM, IDX_DIM, IDX_HEADS, C, C, C, D, D]
    offs = [0]
    for s_ in sp:
        offs.append(offs[-1] + s_)
    seg = lambda a, b_: w[:, offs[a]:offs[b_]]
    assert (2 * D) % C == 0 and ATT_WIDTH == C and IDX_HEADS * IDX_DIM == C
    w_q = seg(0, 1) * (ATT_HEAD_DIM ** -0.5 * LOG2E)
    w_main = jnp.concatenate([w_q, seg(1, 4), seg(9, 11), seg(6, 9)], axis=1)
    n_main = w_main.shape[1]
    tn = _pick(n_main, (1024, 512, 256, 128))
    w_main = w_main.astype(BF16)
    w_small = jnp.pad(seg(4, 6), ((0, 0), (0, LANES - IDX_DIM - IDX_HEADS))).astype(BF16)

    tm_in = _pick(M, (1024, 512, 256, 128))
    z, zs = _inproj(x2, row(ln_in_g), row(ln_in_b), w_main, w_small, tm_in, tn)

    TQ = _pick(S, (256, 128))
    TK = _pick(S, (1024, 512, 256))
    CK = _pick(S, (512, 256))
    ki = zs[:, :IDX_DIM].astype(BF16)
    kk = jnp.concatenate([ki, ki], axis=1)
    wit = jnp.swapaxes(zs[:, IDX_DIM:IDX_DIM + IDX_HEADS].reshape(B, S, IDX_HEADS), 1, 2)
    col_t = lambda c: jnp.swapaxes(z[:, c * ATT_WIDTH:(c + 1) * ATT_WIDTH].reshape(B, S, ATT_WIDTH), 1, 2)
    qt, vt, qit = col_t(0), col_t(2), col_t(3)
    vt = jnp.concatenate([vt.reshape(B, ATT_HEADS, ATT_HEAD_DIM, S),
                          jnp.ones((B, ATT_HEADS, V_PAD, S), BF16)], axis=2).reshape(B, ATT_HEADS * V_ROWS, S)
    tab = _bias_tab(rel_bias.astype(F32), TQ)
    att = _sparse_attn(z, qt, qit, vt, kk, wit, tab, B, S, TQ, TK, CK)

    wr = jnp.pad(w_router[0].astype(F32), ((0, 0), (0, LANES - E)))
    wr_hi = wr.astype(BF16)
    wr_lo = (wr - wr_hi.astype(F32)).astype(BF16)
    br = jnp.pad(b_router[0].astype(F32), (0, LANES - E)).reshape(1, LANES)
    tm_mix = _pick(M, (256, 128))
    h1, h1b, te, tg = _mix(x2, att, z, conv_w[0].astype(F32), w_att_br[0].astype(BF16),
                           w_conv_br[0].astype(BF16), w_o[0].astype(BF16),
                           row(ln_in_g), row(ln_in_b), row(ln1_g[0]), row(ln1_b[0]), wr_hi, wr_lo, br,
                           S, tm_mix, E)

    top_e = te[:, :TOP_K_EXPERTS]
    gates = tg[:, :TOP_K_EXPERTS]
    n_assign = M * TOP_K_EXPERTS
    tm_e = _pick(n_assign, (512, 256, 128))
    e_flat = top_e.reshape(-1)
    order = jnp.argsort(e_flat, stable=True).astype(jnp.int32)
    onehot = (e_flat[:, None] == jnp.arange(E, dtype=jnp.int32)[None, :]).astype(jnp.int32)
    csum = jnp.cumsum(onehot, axis=0)
    counts = csum[-1]
    rank = jnp.take_along_axis(csum, e_flat[:, None], axis=1)[:, 0] - 1
    starts = jnp.cumsum(counts) - counts
    padded = (counts + tm_e - 1) // tm_e * tm_e
    pends = jnp.cumsum(padded)
    pstarts = pends - padded
    pos = pstarts[e_flat] + rank
    n_blocks = (n_assign + E * (tm_e - 1) + tm_e - 1) // tm_e
    n_rows = n_blocks * tm_e
    r = jnp.arange(n_rows, dtype=jnp.int32)
    row_e = jnp.minimum(jnp.sum((r[:, None] >= pends[None, :]).astype(jnp.int32), axis=1), E - 1)
    j = r - pstarts[row_e]
    valid = j < counts[row_e]
    src = jnp.clip(starts[row_e] + j, 0, n_assign - 1)
    a_id = order[src]
    row_tok = jnp.where(valid, a_id // TOP_K_EXPERTS, r % M)
    row_w = jnp.where(valid, gates.reshape(-1)[a_id], 0.0)
    n_used = pends[-1] // tm_e
    blk = jnp.arange(n_blocks, dtype=jnp.int32)
    block_used = (blk < n_used).astype(jnp.int32)
    block_e = row_e.reshape(n_blocks, tm_e)[:, 0][jnp.minimum(blk, n_used - 1)]

    xs = h1b[row_tok]
    rw = jnp.broadcast_to(row_w[:, None], (n_rows, LANES))
    F = w_gate.shape[-1]
    tf = _pick(F, (256, 128))
    block_first = jnp.concatenate([jnp.ones((1,), jnp.int32),
                                   (block_e[1:] != block_e[:-1]).astype(jnp.int32)]) * block_used
    eo = _moe(block_e, block_used, block_first, xs, rw,
              w_gate[0].astype(F32), b_gate[0].reshape(E, 1, F).astype(F32),
              w_up[0].astype(F32), b_up[0].reshape(E, 1, F).astype(F32),
              w_down[0].astype(F32), b_down[0].reshape(E, 1, D).astype(F32), tm_e, tf)

    ffg = eo[pos.reshape(M, TOP_K_EXPERTS).T]

    tm_p = _pick(M, (256, 128))
    out = _ple(h1, h1b, ffg, p[0].reshape(M, -1), w_ple_gate[0].astype(BF16),
               w_ple_proj[0].astype(BF16), row(ln2_g[0]), row(ln2_b[0]), tm_p)
    return out.reshape(B, S, D)
```

```python
import functools
import math

import jax
import jax.numpy as jnp
from jax import lax
from jax.experimental import pallas as pl
from jax.experimental.pallas import tpu as pltpu

CHUNK = 64
CHUNK_SHIFT = 6
assert 1 << CHUNK_SHIFT == CHUNK
ATT_HEADS = 8
ATT_HEAD_DIM = 128
ATT_WIDTH = ATT_HEADS * ATT_HEAD_DIM
IDX_HEADS = 16
IDX_DIM = 64
MAX_TOPK = 256
CONV_WIDTH = 1024
CONV_K = 3
REL_BUCKETS = 32
REL_MAX_DIST = 128
TOP_K_EXPERTS = 4
SWIGLU_LIMIT = 7.0
SWIGLU_ALPHA = 1.702
LN_EPS = 1e-5
DEPTH = 1
DEEPNORM_ALPHA = (2 * DEPTH) ** 0.25

LANES = 128
BF16_ROWS = 16
V_PAD = BF16_ROWS
V_ROWS = ATT_HEAD_DIM + V_PAD
UNCHECKED_PASSES = 24
WIDE_SUBTILES = 4
VMEM_LIMIT_BYTES = 56 * 1024 * 1024

LOG2E = math.log2(math.e)
INT_MIN = -(2 ** 31)
KEY_NEG_INF = (0xFF800000 ^ 0x7FFFFFFF) - (1 << 32)

F32 = jnp.float32
BF16 = jnp.bfloat16


def _cparams(sem):
    return pltpu.CompilerParams(dimension_semantics=sem, vmem_limit_bytes=VMEM_LIMIT_BYTES)


def _ln(xf, g, b):
    mu = jnp.mean(xf, axis=-1, keepdims=True)
    xc = xf - mu
    var = jnp.mean(xc * xc, axis=-1, keepdims=True)
    return xc * lax.rsqrt(var + LN_EPS) * g + b


def _rep_lanes(v, rep):
    return v if rep == 1 else jnp.concatenate([v] * rep, axis=1)


def _resident(shape, index_map):
    return pl.BlockSpec(shape, index_map, pipeline_mode=pl.Buffered(1))


def _bias_tab_kernel(rb_ref, o_ref, *, T):
    c = pl.program_id(0)
    h = pl.program_id(1)
    half = REL_BUCKETS // 2
    max_exact = half // 2
    j = lax.broadcasted_iota(jnp.int32, (T, T), 0)
    i = lax.broadcasted_iota(jnp.int32, (T, T), 1)
    rel = j - i - c * T
    ret = jnp.where(rel > 0, half, 0)
    n = jnp.abs(rel)
    nf = jnp.maximum(n, 1).astype(F32)
    large = max_exact + (jnp.log(nf / max_exact) / math.log(REL_MAX_DIST / max_exact)
                         * (half - max_exact)).astype(jnp.int32)
    large = jnp.minimum(large, half - 1)
    bucket = ret + jnp.where(n < max_exact, n, large)
    far = rb_ref[half - 1, h]
    acc = jnp.zeros((T, T), F32)
    for b in range(REL_BUCKETS):
        acc = jnp.where(bucket == b, rb_ref[b, h] - far, acc)
    o_ref[...] = acc * LOG2E


def _bias_tab(rel_bias, T):
    return pl.pallas_call(
        functools.partial(_bias_tab_kernel, T=T),
        grid=(2, ATT_HEADS),
        in_specs=[pl.BlockSpec(memory_space=pltpu.SMEM)],
        out_specs=pl.BlockSpec((None, None, T, T), lambda c, h: (c, h, 0, 0)),
        out_shape=jax.ShapeDtypeStruct((2, ATT_HEADS, T, T), F32),
        compiler_params=_cparams(("arbitrary", "arbitrary")),
        name="bias_tab",
    )(rel_bias)


def _inproj_kernel(x_ref, g_ref, b_ref, w_ref, ws_ref, z_ref, zs_ref, hn_sc):
    j = pl.program_id(1)

    @pl.when(j == 0)
    def _():
        hn = _ln(x_ref[...], g_ref[...], b_ref[...]).astype(BF16)
        hn_sc[...] = hn
        zs_ref[...] = jnp.dot(hn, ws_ref[...], preferred_element_type=F32)

    z_ref[...] = jnp.dot(hn_sc[...], w_ref[...], preferred_element_type=F32).astype(BF16)


def _inproj(x2, g, b, w_main, w_small, tm, tn):
    M, D = x2.shape
    N = w_main.shape[1]
    return pl.pallas_call(
        _inproj_kernel,
        grid=(M // tm, N // tn),
        in_specs=[
            pl.BlockSpec((tm, D), lambda i, j: (i, 0)),
            pl.BlockSpec((1, D), lambda i, j: (0, 0)),
            pl.BlockSpec((1, D), lambda i, j: (0, 0)),
            pl.BlockSpec((D, tn), lambda i, j: (0, j)),
            pl.BlockSpec((D, LANES), lambda i, j: (0, 0)),
        ],
        out_specs=[
            pl.BlockSpec((tm, tn), lambda i, j: (i, j)),
            pl.BlockSpec((tm, LANES), lambda i, j: (i, 0)),
        ],
        out_shape=[
            jax.ShapeDtypeStruct((M, N), BF16),
            jax.ShapeDtypeStruct((M, LANES), F32),
        ],
        scratch_shapes=[pltpu.VMEM((tm, D), BF16)],
        compiler_params=_cparams(("parallel", "arbitrary")),
        name="ln_inproj",
    )(x2, g, b, w_main, w_small)


def _attn_kernel(qb_ref, kt_ref, qt_ref, qit_ref, em_ref, om_ref, wi_ref, kk_ref, k_ref, vt_ref, tab_ref,
                 o_ref, key_sc, key16_sc, qe_sc, qo_sc, thr_sc, m_sc, acc_sc, madd_sc, x_sc, p_sc,
                 *, TQ, TK, CK, S, topk):
    qb = qb_ref[pl.program_id(1)]
    kt = kt_ref[pl.program_id(1)]
    t0 = qb * TQ
    lim_max = t0 + TQ
    last_kt = (lim_max - 1) // TK
    n_sub = TK // TQ
    SLAB = 32
    SLAB16 = 2 * SLAB

    @pl.when(kt == 0)
    def _phase1():
        qi = qit_ref[...]
        qe_sc[...] = qi * em_ref[...]
        qo_sc[...] = qi * om_ref[...]
        n_ck = (lim_max + CK - 1) // CK
        q_idx = t0 + lax.broadcasted_iota(jnp.int32, (1, TQ), 1)
        limit = ((q_idx >> CHUNK_SHIFT) + 1) << CHUNK_SHIFT

        def score_chunk(c, carry):
            c0 = pl.multiple_of(c * CK, CK)
            kk = kk_ref[pl.ds(c0, CK), :]
            acc = jnp.zeros((CK, TQ), F32)
            for jp in range(IDX_HEADS // 2):
                sl = slice(LANES * jp, LANES * (jp + 1))
                le = jnp.dot(kk, qe_sc[sl, :], preferred_element_type=F32)
                lo = jnp.dot(kk, qo_sc[sl, :], preferred_element_type=F32)
                acc = acc + jnp.maximum(le, 0.0) * wi_ref[2 * jp:2 * jp + 1, :]
                acc = acc + jnp.maximum(lo, 0.0) * wi_ref[2 * jp + 1:2 * jp + 2, :]
            s_idx = c0 + lax.broadcasted_iota(jnp.int32, (CK, TQ), 0)
            sc = jnp.where(s_idx < limit, acc, -jnp.inf)
            bits = pltpu.bitcast(sc, jnp.int32)
            key = bits ^ ((bits >> 31) & 0x7FFFFFFF)
            key_sc[pl.ds(c0, CK), :] = key
            key16_sc[pl.ds(c0, CK), :] = (key >> 16).astype(jnp.int16)
            return carry

        lax.fori_loop(0, n_ck, score_chunk, 0)

        def count(pred):
            def chunk(c, acc):
                c0 = pl.multiple_of(c * CK, CK)
                for r in range(CK // SLAB):
                    blk = key_sc[pl.ds(c0 + SLAB * r, SLAB), :]
                    acc = acc + pred(blk, c0 + SLAB * r).astype(jnp.int32)
                return acc

            acc = lax.fori_loop(0, n_ck, chunk, jnp.zeros((SLAB, TQ), jnp.int32))
            return jnp.sum(acc, axis=0, keepdims=True)

        def bit_pass(i, state, low):
            u, c_u, settled = state
            cand_u = u | jnp.left_shift(jnp.int32(1), 31 - i)
            cand_h = ((cand_u & 0xFFFF) - 0x8000) if low else ((cand_u ^ INT_MIN) >> 16)
            cand16 = jnp.broadcast_to(cand_h.astype(jnp.int16), (SLAB16, TQ))

            def chunk(c, acc):
                c0 = pl.multiple_of(c * CK, CK)
                for r in range(CK // SLAB16):
                    blk = key16_sc[pl.ds(c0 + SLAB16 * r, SLAB16), :]
                    acc = acc + (blk >= cand16).astype(jnp.int16)
                return acc

            acc = lax.fori_loop(0, n_ck, chunk, jnp.zeros((SLAB16, TQ), jnp.int16))
            cnt = jnp.sum(acc.astype(jnp.int32), axis=0, keepdims=True)
            take = cnt >= topk
            settled = jnp.where(cnt == topk, 1, settled)
            return jnp.where(take, cand_u, u), jnp.where(take, cnt, c_u), settled

        settled0 = (limit <= topk).astype(jnp.int32)
        zero = jnp.zeros((1, TQ), jnp.int32)
        state = lax.fori_loop(0, 16, functools.partial(bit_pass, low=False), (zero, zero + topk, settled0))

        t_hi = jnp.broadcast_to((state[0] ^ INT_MIN) >> 16, (SLAB16, TQ))

        def repack_chunk(c, carry):
            c0 = pl.multiple_of(c * CK, CK)
            for r in range(CK // SLAB16):
                rows = pl.ds(c0 + SLAB16 * r, SLAB16)
                key = key_sc[rows, :]
                k_hi = key >> 16
                low16 = jnp.where(k_hi > t_hi, 0x7FFF, jnp.where(k_hi < t_hi, -0x8000, (key & 0xFFFF) - 0x8000))
                key16_sc[rows, :] = low16.astype(jnp.int16)
            return carry

        lax.fori_loop(0, n_ck, repack_chunk, 0)

        low_pass = functools.partial(bit_pass, low=True)

        def checked_pass(s):
            nxt = low_pass(s[0], s[1:4])
            return (s[0] + 1,) + nxt + (jnp.min(nxt[2]),)

        state = lax.fori_loop(16, UNCHECKED_PASSES, low_pass, state)
        state = (jnp.int32(UNCHECKED_PASSES),) + state + (jnp.min(state[2]),)
        _, u, c_u, _, _ = lax.while_loop(lambda s: (s[0] < 32) & (s[4] == 0), checked_pass, state)
        t_star = u ^ INT_MIN
        thr_sc[...] = jnp.broadcast_to(jnp.maximum(t_star, KEY_NEG_INF + 1), thr_sc.shape)

        tie = (c_u > topk) & (t_star > KEY_NEG_INF)

        @pl.when(jnp.max(tie.astype(jnp.int32)) > 0)
        def _ties():
            t_b = jnp.broadcast_to(t_star, (SLAB, TQ))
            need = topk - count(lambda blk, _: blk > t_b)
            row = lax.broadcasted_iota(jnp.int32, (SLAB, TQ), 0)

            def idx_step(i, v):
                cand_v = v | jnp.left_shift(jnp.int32(1), S.bit_length() - 1 - i)
                cand_b = jnp.broadcast_to(cand_v, (SLAB, TQ))
                cnt = count(lambda blk, s0: (blk == t_b) & (row + s0 < cand_b))
                return jnp.where(cnt < need, cand_v, v)

            v = lax.fori_loop(0, S.bit_length(), idx_step, zero)
            drop_from = jnp.broadcast_to(jnp.where(tie, v + 1, S), (SLAB, TQ))

            def fix_chunk(c, carry):
                c0 = pl.multiple_of(c * CK, CK)
                for r in range(CK // SLAB):
                    rows = pl.ds(c0 + SLAB * r, SLAB)
                    blk = key_sc[rows, :]
                    drop = (blk == t_b) & (row + (c0 + SLAB * r) >= drop_from)
                    key_sc[rows, :] = jnp.where(drop, blk - 1, blk)
                return carry

            lax.fori_loop(0, n_ck, fix_chunk, 0)

        m_sc[...] = jnp.full(m_sc.shape, -1e30, F32)
        acc_sc[...] = jnp.zeros(acc_sc.shape, F32)

    def attend(s0, r0, bias_idx, nk=TQ):
        madd_sc[0:nk, :] = jnp.where(key_sc[pl.ds(s0, nk), :] >= thr_sc[0:1, :], 0.0, -jnp.inf)

        def logits(h):
            hs = slice(ATT_HEAD_DIM * h, ATT_HEAD_DIM * (h + 1))
            x = jnp.dot(k_ref[pl.ds(r0, nk), hs], qt_ref[hs, :],
                        preferred_element_type=F32) + madd_sc[0:nk, :]
            if bias_idx is not None:
                x = x + tab_ref[bias_idx, h]
            x_sc[h % 3, 0:nk, :] = x
            return jnp.max(x, axis=0, keepdims=True)

        def weighted_values(h, alpha):
            hv = slice(V_ROWS * h, V_ROWS * (h + 1))
            pv = jnp.dot(vt_ref[hv, pl.ds(r0, nk)], p_sc[h % 2, 0:nk, :], preferred_element_type=F32)
            acc_sc[hv, :] = acc_sc[hv, :] * alpha + pv

        m_cur = {0: logits(0), 1: logits(1)}
        alphas = {}
        for h in range(ATT_HEADS):
            if h + 2 < ATT_HEADS:
                m_cur[h + 2] = logits(h + 2)
            if h >= 1:
                weighted_values(h - 1, alphas.pop(h - 1))
            m_old = m_sc[h:h + 1, :]
            m_new = jnp.maximum(m_old, m_cur.pop(h))
            alpha = jnp.exp2(m_old - m_new)
            p_sc[h % 2, 0:nk, :] = jnp.exp2(x_sc[h % 3, 0:nk, :] - m_new).astype(BF16)
            m_sc[h:h + 1, :] = m_new
            alphas[h] = alpha
        weighted_values(ATT_HEADS - 1, alphas.pop(ATT_HEADS - 1))

    wide = min(WIDE_SUBTILES * TQ, TK)
    all_far = (kt + 1) * TK <= t0 - TQ

    @pl.when(all_far)
    def _phase2_far():
        def sub(c, carry):
            r0 = pl.multiple_of(c * wide, wide)
            attend(pl.multiple_of(kt * TK + r0, wide), r0, None, wide)
            return carry

        lax.fori_loop(0, TK // wide, sub, 0)

    @pl.when(jnp.logical_not(all_far))
    def _phase2():
        def sub(c, carry):
            r0 = pl.multiple_of(c * TQ, TQ)
            s0 = pl.multiple_of(kt * TK + r0, TQ)
            d = s0 - t0

            @pl.when(d == 0)
            def _():
                attend(s0, r0, 0)

            @pl.when(d == -TQ)
            def _():
                attend(s0, r0, 1)

            @pl.when(d < -TQ)
            def _():
                attend(s0, r0, None)

            return carry

        lax.fori_loop(0, n_sub, sub, 0)

    @pl.when(kt == last_kt)
    def _finish():
        for h in range(ATT_HEADS):
            hs = slice(ATT_HEAD_DIM * h, ATT_HEAD_DIM * (h + 1))
            num = acc_sc[V_ROWS * h:V_ROWS * h + ATT_HEAD_DIM, :]
            den = acc_sc[V_ROWS * h + ATT_HEAD_DIM:V_ROWS * h + ATT_HEAD_DIM + 1, :]
            o_ref[:, hs] = (num / den).T.astype(o_ref.dtype)


def _sparse_attn(z, qt, qit, vt, kk, wit, tab, B, S, TQ, TK, CK):
    nq = S // TQ
    nk = S // TK
    topk = min(MAX_TOPK, S // 4)
    wide = min(WIDE_SUBTILES * TQ, TK)
    feat =jnp.arange(ATT_WIDTH, dtype=jnp.int32)[:, None]
    em = jnp.broadcast_to(((feat // IDX_DIM) % 2 == 0).astype(BF16), (ATT_WIDTH, TQ))
    om = jnp.broadcast_to(((feat // IDX_DIM) % 2 == 1).astype(BF16), (ATT_WIDTH, TQ))

    pairs = [(i, k) for i in range(nq) for k in range(((i + 1) * TQ - 1) // TK + 1)]
    qb_tab = jnp.asarray([i for i, _ in pairs], jnp.int32)
    kt_tab = jnp.asarray([k for _, k in pairs], jnp.int32)

    kern = functools.partial(_attn_kernel, TQ=TQ, TK=TK, CK=CK, S=S, topk=topk)
    grid_spec = pltpu.PrefetchScalarGridSpec(
        num_scalar_prefetch=2,
        grid=(B, len(pairs)),
        in_specs=[
            pl.BlockSpec((None, ATT_WIDTH, TQ), lambda b, s, qb, kt: (b, 0, qb[s])),
            pl.BlockSpec((None, ATT_WIDTH, TQ), lambda b, s, qb, kt: (b, 0, qb[s])),
            _resident((ATT_WIDTH, TQ), lambda b, s, qb, kt: (0, 0)),
            _resident((ATT_WIDTH, TQ), lambda b, s, qb, kt: (0, 0)),
            pl.BlockSpec((None, IDX_HEADS, TQ), lambda b, s, qb, kt: (b, 0, qb[s])),
            pl.BlockSpec((S, LANES), lambda b, s, qb, kt: (b, 0), pipeline_mode=pl.Buffered(1)),
            pl.BlockSpec((TK, ATT_WIDTH), lambda b, s, qb, kt: (b * nk + kt[s], 1)),
            pl.BlockSpec((None, ATT_HEADS * V_ROWS, TK), lambda b, s, qb, kt: (b, 0, kt[s])),
            _resident((2, ATT_HEADS, TQ, TQ), lambda b, s, qb, kt: (0, 0, 0, 0)),
        ],
        out_specs=pl.BlockSpec((TQ, ATT_WIDTH), lambda b, s, qb, kt: (b * nq + qb[s], 0)),
        scratch_shapes=[
            pltpu.VMEM((S, TQ), jnp.int32),
            pltpu.VMEM((S, TQ), jnp.int16),
            pltpu.VMEM((ATT_WIDTH, TQ), BF16),
            pltpu.VMEM((ATT_WIDTH, TQ), BF16),
            pltpu.VMEM((8, TQ), jnp.int32),
            pltpu.VMEM((ATT_HEADS, TQ), F32),
            pltpu.VMEM((ATT_HEADS * V_ROWS, TQ), F32),
            pltpu.VMEM((wide, TQ), F32),
            pltpu.VMEM((3, wide, TQ), F32),
            pltpu.VMEM((2, wide, TQ), BF16),
        ],
    )
    return pl.pallas_call(
        kern,
        grid_spec=grid_spec,
        out_shape=jax.ShapeDtypeStruct((B * S, ATT_WIDTH), BF16),
        compiler_params=_cparams(("parallel", "arbitrary")),
        name="sparse_attn",
    )(qb_tab, kt_tab, qt, qit, em, om, wit, kk, z, vt, tab)


def _mix_kernel(x_ref, att_ref, gb_ref, gc_ref, u_ref, gch_ref, uh_ref, ga_ref, gv_ref,
                cw_ref, wa_ref, wb_ref, wo_ref, ling_ref, linb_ref, l1g_ref, l1b_ref,
                wrh_ref, wrl_ref, br_ref, h1_ref, h1b_ref, te_ref, tg_ref, *, tm, S, HALO, n_exp):
    i = pl.program_id(0)
    seq_start = (i * tm) % S == 0
    cu = gc_ref[...].astype(F32) * u_ref[...].astype(F32)
    cuh = gch_ref[...].astype(F32) * uh_ref[...].astype(F32)
    cuh = jnp.where(seq_start, 0.0, cuh)
    ext = jnp.concatenate([cuh, cu], axis=0)
    y = (cw_ref[2:3, :] * cu
         + cw_ref[1:2, :] * ext[HALO - 1:HALO - 1 + tm]
         + cw_ref[0:1, :] * ext[HALO - 2:HALO - 2 + tm])
    conv = (gb_ref[...].astype(F32) * y).astype(BF16)
    a = jnp.dot(att_ref[...], wa_ref[...], preferred_element_type=F32)
    c = jnp.dot(conv, wb_ref[...], preferred_element_type=F32)
    merged = (jax.nn.sigmoid(ga_ref[...].astype(F32)) * a
              + jax.nn.sigmoid(gv_ref[...].astype(F32)) * c).astype(BF16)
    mix = jnp.dot(merged, wo_ref[...], preferred_element_type=F32)
    h = _ln(x_ref[...], ling_ref[...], linb_ref[...])
    h1 = _ln(DEEPNORM_ALPHA * h + mix, l1g_ref[...], l1b_ref[...])
    h1_ref[...] = h1
    h1b_ref[...] = h1.astype(BF16)

    h1_hi = h1.astype(BF16)
    h1_lo = (h1 - h1_hi.astype(F32)).astype(BF16)
    logits = (jnp.dot(h1_hi, wrh_ref[...], preferred_element_type=F32)
              + (jnp.dot(h1_hi, wrl_ref[...], preferred_element_type=F32)
                 + jnp.dot(h1_lo, wrh_ref[...], preferred_element_type=F32))) + br_ref[...]
    lane = lax.broadcasted_iota(jnp.int32, (tm, LANES), 1)
    work = jnp.where(lane < n_exp, logits, -jnp.inf)
    te = jnp.zeros((tm, LANES), jnp.int32)
    tv = jnp.zeros((tm, LANES), F32)
    v0 = None
    den = jnp.zeros((tm, 1), F32)
    for k in range(TOP_K_EXPERTS):
        vk = jnp.max(work, axis=1, keepdims=True)
        ik = jnp.min(jnp.where(work == vk, lane, LANES), axis=1, keepdims=True)
        if k == 0:
            v0 = vk
        ek = jnp.exp(vk - v0)
        den = den + ek
        te = jnp.where(lane == k, ik, te)
        tv = jnp.where(lane == k, ek, tv)
        work = jnp.where(lane == ik, -jnp.inf, work)
    te_ref[...] = te
    tg_ref[...] = tv / den


def _mix(x2, att, z, conv_w, wa, wb, wo, ling, linb, l1g, l1b, wr_hi, wr_lo, br, S, tm, n_exp):
    M, D = x2.shape
    C = CONV_WIDTH
    HALO = 16
    hb = tm // HALO
    ga_blk = 4 * C // D
    gb_blk = (4 * C + 2 * D) // C
    kern = functools.partial(_mix_kernel, tm=tm, S=S, HALO=HALO, n_exp=n_exp)
    halo_map = lambda col: (lambda i: (jnp.maximum(i * hb - 1, 0), col))
    return pl.pallas_call(
        kern,
        grid=(M // tm,),
        in_specs=[
            pl.BlockSpec((tm, D), lambda i: (i, 0)),
            pl.BlockSpec((tm, ATT_WIDTH), lambda i: (i, 0)),
            pl.BlockSpec((tm, C), lambda i: (i, gb_blk)),
            pl.BlockSpec((tm, C), lambda i: (i, gb_blk + 1)),
            pl.BlockSpec((tm, C), lambda i: (i, gb_blk + 2)),
            pl.BlockSpec((HALO, C), halo_map(gb_blk + 1)),
            pl.BlockSpec((HALO, C), halo_map(gb_blk + 2)),
            pl.BlockSpec((tm, D), lambda i: (i, ga_blk)),
            pl.BlockSpec((tm, D), lambda i: (i, ga_blk + 1)),
            _resident((CONV_K, C), lambda i: (0, 0)),
            _resident((ATT_WIDTH, D), lambda i: (0, 0)),
            _resident((C, D), lambda i: (0, 0)),
            _resident((D, D), lambda i: (0, 0)),
            _resident((1, D), lambda i: (0, 0)),
            _resident((1, D), lambda i: (0, 0)),
            _resident((1, D), lambda i: (0, 0)),
            _resident((1, D), lambda i: (0, 0)),
            _resident((D, LANES), lambda i: (0, 0)),
            _resident((D, LANES), lambda i: (0, 0)),
            _resident((1, LANES), lambda i: (0, 0)),
        ],
        out_specs=[
            pl.BlockSpec((tm, D), lambda i: (i, 0)),
            pl.BlockSpec((tm, D), lambda i: (i, 0)),
            pl.BlockSpec((tm, LANES), lambda i: (i, 0)),
            pl.BlockSpec((tm, LANES), lambda i: (i, 0)),
        ],
        out_shape=[
            jax.ShapeDtypeStruct((M, D), F32),
            jax.ShapeDtypeStruct((M, D), BF16),
            jax.ShapeDtypeStruct((M, LANES), jnp.int32),
            jax.ShapeDtypeStruct((M, LANES), F32),
        ],
        compiler_params=_cparams(("parallel",)),
        name="mix_ln1",
    )(x2, att, z, z, z, z, z, z, z, conv_w, wa, wb, wo, ling, linb, l1g, l1b, wr_hi, wr_lo, br)


def _moe_kernel(be_ref, bv_ref, bf_ref, xs_ref, rw_ref, wg_ref, bg_ref, wu_ref, bu_ref, wd_ref, bd_ref,
                o_ref, acc_sc, wg_sc, wu_sc, wd_sc):
    i = pl.program_id(0)
    f = pl.program_id(1)
    last_f = pl.num_programs(1) - 1
    tf = wg_ref.shape[1]
    used = bv_ref[i] > 0

    @pl.when(bf_ref[i] > 0)
    def _():
        wg_sc[f] = wg_ref[...].astype(BF16)
        wu_sc[f] = wu_ref[...].astype(BF16)
        wd_sc[f] = wd_ref[...].astype(BF16)

    @pl.when(used)
    def _():
        @pl.when(f == 0)
        def _():
            acc_sc[...] = jnp.zeros(acc_sc.shape, F32)

        xb = xs_ref[...]
        cols = pl.ds(pl.multiple_of(f * tf, tf), tf)
        g = jnp.dot(xb, wg_sc[f], preferred_element_type=F32) + bg_ref[:, cols]
        u = jnp.dot(xb, wu_sc[f], preferred_element_type=F32) + bu_ref[:, cols]
        g = jnp.minimum(g, SWIGLU_LIMIT)
        u = jnp.clip(u, -SWIGLU_LIMIT, SWIGLU_LIMIT)
        a = (u + 1.0) * (g * jax.nn.sigmoid(SWIGLU_ALPHA * g))
        acc_sc[...] += jnp.dot(a.astype(BF16), wd_sc[f], preferred_element_type=F32)

        @pl.when(f == last_f)
        def _():
            o_ref[...] = ((acc_sc[...] + bd_ref[...]) * rw_ref[:, 0:1]).astype(o_ref.dtype)

    @pl.when(jnp.logical_not(used) & (f == last_f))
    def _():
        o_ref[...] = jnp.zeros(o_ref.shape, o_ref.dtype)


def _moe(block_e, block_used, block_first, xs, rw, wg, bg, wu, bu, wd, bd, tm, tf):
    R, D = xs.shape
    F = wg.shape[2]
    nf = F // tf

    def f_blk(i, f, bf):
        return jnp.where(bf[i] > 0, f, nf - 1)

    grid_spec = pltpu.PrefetchScalarGridSpec(
        num_scalar_prefetch=3,
        grid=(R // tm, nf),
        in_specs=[
            pl.BlockSpec((tm, D), lambda i, f, be, bv, bf: (i, 0)),
            pl.BlockSpec((tm, LANES), lambda i, f, be, bv, bf: (i, 0)),
            pl.BlockSpec((None, D, tf), lambda i, f, be, bv, bf: (be[i], 0, f_blk(i, f, bf))),
            pl.BlockSpec((None, 1, F), lambda i, f, be, bv, bf: (be[i], 0, 0)),
            pl.BlockSpec((None, D, tf), lambda i, f, be, bv, bf: (be[i], 0, f_blk(i, f, bf))),
            pl.BlockSpec((None, 1, F), lambda i, f, be, bv, bf: (be[i], 0, 0)),
            pl.BlockSpec((None, tf, D), lambda i, f, be, bv, bf: (be[i], f_blk(i, f, bf), 0)),
            pl.BlockSpec((None, 1, D), lambda i, f, be, bv, bf: (be[i], 0, 0)),
        ],
        out_specs=pl.BlockSpec((tm, D), lambda i, f, be, bv, bf: (i, 0)),
        scratch_shapes=[
            pltpu.VMEM((tm, D), F32),
            pltpu.VMEM((nf, D, tf), BF16),
            pltpu.VMEM((nf, D, tf), BF16),
            pltpu.VMEM((nf, tf, D), BF16),
        ],
    )
    return pl.pallas_call(
        _moe_kernel,
        grid_spec=grid_spec,
        out_shape=jax.ShapeDtypeStruct((R, D), BF16),
        compiler_params=_cparams(("arbitrary", "arbitrary")),
        name="moe_experts",
    )(block_e, block_used, block_first, xs, rw, wg, bg, wu, bu, wd, bd)


def _ple_kernel(h1_ref, h1b_ref, p_ref, wpg_ref, wpp_ref, o_ref):
    gate = jax.nn.sigmoid(jnp.dot(h1b_ref[...], wpg_ref[...], preferred_element_type=F32))
    proj = jnp.dot(p_ref[...].astype(BF16), wpp_ref[...], preferred_element_type=F32)
    o_ref[...] = DEEPNORM_ALPHA * h1_ref[...] + gate * proj


def _ln2_kernel(r_ref, ff_ref, g_ref, b_ref, o_ref):
    ff = ff_ref[0].astype(F32)
    for k in range(1, TOP_K_EXPERTS):
        ff = ff + ff_ref[k].astype(F32)
    o_ref[...] = _ln(r_ref[...] + ff, g_ref[...], b_ref[...])


def _ple(h1, h1b, p2, wpg, wpp, tm):
    M, D = h1.shape
    P = p2.shape[1]
    return pl.pallas_call(
        _ple_kernel,
        grid=(M // tm,),
        in_specs=[
            pl.BlockSpec((tm, D), lambda i: (i, 0)),
            pl.BlockSpec((tm, D), lambda i: (i, 0)),
            pl.BlockSpec((tm, P), lambda i: (i, 0)),
            _resident((D, D), lambda i: (0, 0)),
            _resident((P, D), lambda i: (0, 0)),
        ],
        out_specs=pl.BlockSpec((tm, D), lambda i: (i, 0)),
        out_shape=jax.ShapeDtypeStruct((M, D), F32),
        compiler_params=_cparams(("parallel",)),
        name="ple_residual",
    )(h1, h1b, p2, wpg, wpp)


def _combine_ln2(res, ffg, g, b, tm):
    M, D = res.shape
    return pl.pallas_call(
        _ln2_kernel,
        grid=(M // tm,),
        in_specs=[
            pl.BlockSpec((tm, D), lambda i: (i, 0)),
            pl.BlockSpec((TOP_K_EXPERTS, tm, D), lambda i: (0, i, 0)),
            _resident((1, D), lambda i: (0, 0)),
            _resident((1, D), lambda i: (0, 0)),
        ],
        out_specs=pl.BlockSpec((tm, D), lambda i: (i, 0)),
        out_shape=jax.ShapeDtypeStruct((M, D), F32),
        compiler_params=_cparams(("parallel",)),
        name="combine_ln2",
    )(res, ffg, g, b)


def _pick(n, prefs):
    for t in prefs:
        if n % t == 0:
            return t
    return n


def kernel(x, p, ln_in_g, ln_in_b, rel_bias, w_in, conv_w, w_att_br, w_conv_br, w_o, ln1_g, ln1_b,
           w_router, b_router, w_gate, b_gate, w_up, b_up, w_down, b_down, w_ple_gate, w_ple_proj,
           ln2_g, ln2_b):
    B, S, D = x.shape
    M = B * S
    E = w_router.shape[-1]
    assert w_in.shape[0] == DEPTH == 1
    C = CONV_WIDTH
    x2 = x.reshape(M, D)
    row = lambda v: v.reshape(1, -1).astype(F32)

    w = w_in[0]
    sp = [ATT_WIDTH, ATT_WIDTH, ATT_WIDTH, IDX_HEADS * IDX_DIM, IDX_DIM, IDX_HEADS, C, C, C, D, D]
    offs = [0]
    for s_ in sp:
        offs.append(offs[-1] + s_)
    seg = lambda a, b_: w[:, offs[a]:offs[b_]]
    assert (2 * D) % C == 0 and ATT_WIDTH == C and IDX_HEADS * IDX_DIM == C
    w_q = seg(0, 1) * (ATT_HEAD_DIM ** -0.5 * LOG2E)
    w_main = jnp.concatenate([w_q, seg(1, 4), seg(9, 11), seg(6, 9)], axis=1)
    n_main = w_main.shape[1]
    tn = _pick(n_main, (1024, 512, 256, 128))
    w_main = w_main.astype(BF16)
    w_small = jnp.pad(seg(4, 6), ((0, 0), (0, LANES - IDX_DIM - IDX_HEADS))).astype(BF16)

    tm_in = _pick(M, (1024, 512, 256, 128))
    z, zs = _inproj(x2, row(ln_in_g), row(ln_in_b), w_main, w_small, tm_in, tn)

    TQ = _pick(S, (256, 128))
    TK = _pick(S, (1024, 512, 256))
    CK = _pick(S, (512, 256))
    ki = zs[:, :IDX_DIM].astype(BF16)
    kk = jnp.concatenate([ki, ki], axis=1)
    wit = jnp.swapaxes(zs[:, IDX_DIM:IDX_DIM + IDX_HEADS].reshape(B, S, IDX_HEADS), 1, 2)
    col_t = lambda c: jnp.swapaxes(z[:, c * ATT_WIDTH:(c + 1) * ATT_WIDTH].reshape(B, S, ATT_WIDTH), 1, 2)
    qt, vt, qit = col_t(0), col_t(2), col_t(3)
    vt = jnp.concatenate([vt.reshape(B, ATT_HEADS, ATT_HEAD_DIM, S),
                          jnp.ones((B, ATT_HEADS, V_PAD, S), BF16)], axis=2).reshape(B, ATT_HEADS * V_ROWS, S)
    tab = _bias_tab(rel_bias.astype(F32), TQ)
    att = _sparse_attn(z, qt, qit, vt, kk, wit, tab, B, S, TQ, TK, CK)

    wr = jnp.pad(w_router[0].astype(F32), ((0, 0), (0, LANES - E)))
    wr_hi = wr.astype(BF16)
    wr_lo = (wr - wr_hi.astype(F32)).astype(BF16)
    br = jnp.pad(b_router[0].astype(F32), (0, LANES - E)).reshape(1, LANES)
    tm_mix = _pick(M, (256, 128))
    h1, h1b, te, tg = _mix(x2, att, z, conv_w[0].astype(F32), w_att_br[0].astype(BF16),
                           w_conv_br[0].astype(BF16), w_o[0].astype(BF16),
                           row(ln_in_g), row(ln_in_b), row(ln1_g[0]), row(ln1_b[0]), wr_hi, wr_lo, br,
                           S, tm_mix, E)

    top_e = te[:, :TOP_K_EXPERTS]
    gates = tg[:, :TOP_K_EXPERTS]
    n_assign = M * TOP_K_EXPERTS
    tm_e = _pick(n_assign, (512, 256, 128))
    e_flat = top_e.reshape(-1)
    order = jnp.argsort(e_flat, stable=True).astype(jnp.int32)
    onehot = (e_flat[:, None] == jnp.arange(E, dtype=jnp.int32)[None, :]).astype(jnp.int32)
    csum = jnp.cumsum(onehot, axis=0)
    counts = csum[-1]
    rank = jnp.take_along_axis(csum, e_flat[:, None], axis=1)[:, 0] - 1
    starts = jnp.cumsum(counts) - counts
    padded = (counts + tm_e - 1) // tm_e * tm_e
    pends = jnp.cumsum(padded)
    pstarts = pends - padded
    pos = pstarts[e_flat] + rank
    n_blocks = (n_assign + E * (tm_e - 1) + tm_e - 1) // tm_e
    n_rows = n_blocks * tm_e
    r = jnp.arange(n_rows, dtype=jnp.int32)
    row_e = jnp.minimum(jnp.sum((r[:, None] >= pends[None, :]).astype(jnp.int32), axis=1), E - 1)
    j = r - pstarts[row_e]
    valid = j < counts[row_e]
    src = jnp.clip(starts[row_e] + j, 0, n_assign - 1)
    a_id = order[src]
    row_tok = jnp.where(valid, a_id // TOP_K_EXPERTS, r % M)
    row_w = jnp.where(valid, gates.reshape(-1)[a_id], 0.0)
    n_used = pends[-1] // tm_e
    blk = jnp.arange(n_blocks, dtype=jnp.int32)
    block_used = (blk < n_used).astype(jnp.int32)
    block_e = row_e.reshape(n_blocks, tm_e)[:, 0][jnp.minimum(blk, n_used - 1)]

    xs = h1b[row_tok]
    rw = jnp.broadcast_to(row_w[:, None], (n_rows, LANES))
    F = w_gate.shape[-1]
    tf = _pick(F, (256, 128))
    block_first = jnp.concatenate([jnp.ones((1,), jnp.int32),
                                   (block_e[1:] != block_e[:-1]).astype(jnp.int32)]) * block_used
    eo = _moe(block_e, block_used, block_first, xs, rw,
              w_gate[0].astype(F32), b_gate[0].reshape(E, 1, F).astype(F32),
              w_up[0].astype(F32), b_up[0].reshape(E, 1, F).astype(F32),
              w_down[0].astype(F32), b_down[0].reshape(E, 1, D).astype(F32), tm_e, tf)

    ffg = eo[pos.reshape(M, TOP_K_EXPERTS).T]

    tm_p = _pick(M, (256, 128))
    res = _ple(h1, h1b, p[0].reshape(M, -1), w_ple_gate[0].astype(BF16), w_ple_proj[0].astype(BF16), tm_p)
    out = _combine_ln2(res, ffg, row(ln2_g[0]), row(ln2_b[0]), _pick(M, (512, 256, 128)))
    return out.reshape(B, S, D)
```
